```python
import math
import functools
import jax
import jax.numpy as jnp
from jax import lax
import numpy as np

D_MODEL = 2048
BATCH = 8
SEQ = 2048
DEPTH = 1
DEC_BATCH = 32
DEC_SEQ = 1
PAST_LEN = 8192
PAGE_SIZE = 128

N_HEADS = 8
HEAD_DIM = D_MODEL // N_HEADS // 2
D_ATTN = N_HEADS * 2 * HEAD_DIM
SCALE = HEAD_DIM ** -0.5
D_CONV = D_MODEL
CONV_WIDTH = 3
D_FF = ((-(-8 * D_MODEL // 3) + 255) // 256) * 256
N_IN = 3 * D_CONV + 3 * D_ATTN + 2 * D_MODEL
SPLIT_POINTS = (D_CONV, 2 * D_CONV, 3 * D_CONV,
                3 * D_CONV + D_ATTN, 3 * D_CONV + 2 * D_ATTN, 3 * D_CONV + 3 * D_ATTN,
                3 * D_CONV + 3 * D_ATTN + D_MODEL)
Q_BLOCK = 128
EPS = 1e-6
NEG = -1e30

kernel_name = "gated_parallel_shortconv_diffattn_decoder_step"


def rmsnorm(x, g):
    xf = x.astype(jnp.float32)
    y = xf * lax.rsqrt(jnp.mean(xf * xf, axis=-1, keepdims=True) + EPS)
    return (y * g.astype(jnp.float32)).astype(x.dtype)


def diff_lambda(lq1, lk1, lq2, lk2, lam_init):
    f = lambda a: a.astype(jnp.float32)
    return jnp.exp(jnp.sum(f(lq1) * f(lk1))) - jnp.exp(jnp.sum(f(lq2) * f(lk2))) + lam_init


def diff_attn_prompt(q, k, v, lam):
    B, S = q.shape[:2]
    nb = S // Q_BLOCK
    qb = q.reshape(B, nb, Q_BLOCK, N_HEADS, 2 * HEAD_DIM).transpose(1, 0, 2, 3, 4)
    k1, k2 = k[..., :HEAD_DIM], k[..., HEAD_DIM:]
    kpos = jnp.arange(S)

    def block(args):
        qi, i = args
        s1 = jnp.einsum('bqhd,bkhd->bhqk', qi[..., :HEAD_DIM], k1, preferred_element_type=jnp.float32)
        s2 = jnp.einsum('bqhd,bkhd->bhqk', qi[..., HEAD_DIM:], k2, preferred_element_type=jnp.float32)
        qpos = i * Q_BLOCK + jnp.arange(Q_BLOCK)
        mask = kpos[None, :] <= qpos[:, None]
        p = (jax.nn.softmax(jnp.where(mask, s1, NEG), axis=-1)
             - lam * jax.nn.softmax(jnp.where(mask, s2, NEG), axis=-1))
        return jnp.einsum('bhqk,bkhd->bqhd', p.astype(v.dtype), v, preferred_element_type=jnp.float32)

    o = lax.map(block, (qb, jnp.arange(nb)))
    return o.transpose(1, 0, 2, 3, 4).reshape(B, S, N_HEADS, 2 * HEAD_DIM)


def diff_attn_sample(q, k_new, v_new, cache_k, cache_v, page_table, layer, lam):
    DB, T = q.shape[:2]
    n_pages = page_table.shape[1]
    past = n_pages * PAGE_SIZE
    q1, q2 = q[..., :HEAD_DIM], q[..., HEAD_DIM:]

    def page_scores(j):
        kp = cache_k[layer, page_table[:, j]]
        s1 = jnp.einsum('bqhd,bkhd->bhqk', q1, kp[..., :HEAD_DIM].astype(q.dtype), preferred_element_type=jnp.float32)
        s2 = jnp.einsum('bqhd,bkhd->bhqk', q2, kp[..., HEAD_DIM:].astype(q.dtype), preferred_element_type=jnp.float32)
        return s1, s2

    s1p, s2p = lax.map(page_scores, jnp.arange(n_pages))
    flat = lambda s: s.transpose(1, 2, 3, 0, 4).reshape(DB, N_HEADS, T, past)
    causal = jnp.tril(jnp.ones((T, T), dtype=bool))
    s1n = jnp.where(causal, jnp.einsum('bqhd,bkhd->bhqk', q1, k_new[..., :HEAD_DIM], preferred_element_type=jnp.float32), NEG)
    s2n = jnp.where(causal, jnp.einsum('bqhd,bkhd->bhqk', q2, k_new[..., HEAD_DIM:], preferred_element_type=jnp.float32), NEG)
    p = (jax.nn.softmax(jnp.concatenate([flat(s1p), s1n], axis=-1), axis=-1)
         - lam * jax.nn.softmax(jnp.concatenate([flat(s2p), s2n], axis=-1), axis=-1))
    p_past = p[..., :past].reshape(DB, N_HEADS, T, n_pages, PAGE_SIZE).transpose(3, 0, 1, 2, 4)
    o0 = jnp.einsum('bhqk,bkhd->bqhd', p[..., past:].astype(v_new.dtype), v_new, preferred_element_type=jnp.float32)

    def acc(o, xs):
        pj, j = xs
        vp = cache_v[layer, page_table[:, j]]
        return o + jnp.einsum('bhqk,bkhd->bqhd', pj.astype(vp.dtype), vp, preferred_element_type=jnp.float32), None

    o, _ = lax.scan(acc, o0, (p_past, jnp.arange(n_pages)))
    return o


def trunk_layer(x, conv_buf, attend, w_in, conv_w, w_out_conv, lam_init, subln_g,
                w_out_attn, w_o, norm_mix_g, norm_ffn_g, w_gate, w_up, w_down):
    B, T, _ = x.shape
    hn = rmsnorm(x, norm_mix_g)
    z = hn @ w_in
    b_g, c_g, h_c, q, k, v, g_a, g_b = jnp.split(z, SPLIT_POINTS, axis=-1)
    u = c_g * h_c
    full = jnp.concatenate([conv_buf.astype(u.dtype), u], axis=1)
    yc = full[:, 0:T] * conv_w[:, 0]
    for j in range(1, CONV_WIDTH):
        yc = yc + full[:, j:j + T] * conv_w[:, j]
    new_buf = full[:, T:]
    y_a = (b_g * yc) @ w_out_conv
    q = q.reshape(B, T, N_HEADS, 2 * HEAD_DIM) * SCALE
    k = k.reshape(B, T, N_HEADS, 2 * HEAD_DIM)
    v = v.reshape(B, T, N_HEADS, 2 * HEAD_DIM)
    o = rmsnorm(attend(q, k, v), subln_g) * (1.0 - lam_init)
    y_b = o.astype(x.dtype).reshape(B, T, D_ATTN) @ w_out_attn
    h = x + (jax.nn.sigmoid(g_a) * y_a + jax.nn.sigmoid(g_b) * y_b) @ w_o
    f = rmsnorm(h, norm_ffn_g)
    out = h + (jax.nn.silu(f @ w_gate) * (f @ w_up)) @ w_down
    return out, k, v, new_buf


def setup_inputs(seed: int = 0) -> dict:
    key = jax.random.key(seed)
    ks = jax.random.split(key, 24)
    n_pages = PAST_LEN // PAGE_SIZE
    n_used = DEC_BATCH * n_pages
    n_phys = n_used + (n_used + 3) // 4
    nrm = lambda k, shape, s: jax.random.normal(k, shape, jnp.float32) * s
    page_table = jax.random.permutation(ks[5], n_phys)[:n_used].reshape(DEC_BATCH, n_pages).astype(jnp.int32)
    return {
        "x_prompt": nrm(ks[0], (BATCH, SEQ, D_MODEL), 1.0),
        "x_sample": nrm(ks[1], (DEC_BATCH, DEC_SEQ, D_MODEL), 1.0),
        "cache_k": nrm(ks[2], (DEPTH, n_phys, PAGE_SIZE, N_HEADS, 2 * HEAD_DIM), 1.0),
        "cache_v": nrm(ks[3], (DEPTH, n_phys, PAGE_SIZE, N_HEADS, 2 * HEAD_DIM), 1.0),
        "state_conv": nrm(ks[4], (DEPTH, DEC_BATCH, CONV_WIDTH - 1, D_CONV), 1.0),
        "page_table": page_table,
        "w_in": nrm(ks[6], (DEPTH, D_MODEL, N_IN), D_MODEL ** -0.5),
        "conv_w": nrm(ks[7], (DEPTH, D_CONV, CONV_WIDTH), CONV_WIDTH ** -0.5),
        "w_out_conv": nrm(ks[8], (DEPTH, D_CONV, D_MODEL), D_CONV ** -0.5),
        "lambda_q1": nrm(ks[9], (DEPTH, HEAD_DIM), 0.1),
        "lambda_k1": nrm(ks[10], (DEPTH, HEAD_DIM), 0.1),
        "lambda_q2": nrm(ks[11], (DEPTH, HEAD_DIM), 0.1),
        "lambda_k2": nrm(ks[12], (DEPTH, HEAD_DIM), 0.1),
        "subln_g": 1.0 + nrm(ks[13], (DEPTH, 2 * HEAD_DIM), 0.02),
        "w_out_attn": nrm(ks[14], (DEPTH, D_ATTN, D_MODEL), D_ATTN ** -0.5),
        "w_o": nrm(ks[15], (DEPTH, D_MODEL, D_MODEL), D_MODEL ** -0.5),
        "norm_mix_g": 1.0 + nrm(ks[16], (DEPTH, D_MODEL), 0.02),
        "norm_ffn_g": 1.0 + nrm(ks[17], (DEPTH, D_MODEL), 0.02),
        "w_gate": nrm(ks[18], (DEPTH, D_MODEL, D_FF), D_MODEL ** -0.5),
        "w_up": nrm(ks[19], (DEPTH, D_MODEL, D_FF), D_MODEL ** -0.5),
        "w_down": nrm(ks[20], (DEPTH, D_FF, D_MODEL), D_FF ** -0.5),
        "norm_final_g": 1.0 + nrm(ks[21], (D_MODEL,), 0.02),
    }


def reference(x_prompt, x_sample, cache_k, cache_v, state_conv, page_table, w_in, conv_w,
              w_out_conv, lambda_q1, lambda_k1, lambda_q2, lambda_k2, subln_g, w_out_attn,
              w_o, norm_mix_g, norm_ffn_g, w_gate, w_up, w_down, norm_final_g):
    xp, xs = x_prompt, x_sample
    zero_buf = jnp.zeros((xp.shape[0], CONV_WIDTH - 1, D_CONV), xp.dtype)
    kp_l, vp_l, cp_l, ks_l, vs_l, cs_l = [], [], [], [], [], []
    for l in range(DEPTH):
        lam_init = 0.8 - 0.6 * math.exp(-0.3 * l)
        lam = diff_lambda(lambda_q1[l], lambda_k1[l], lambda_q2[l], lambda_k2[l], lam_init)
        shared = (w_in[l], conv_w[l], w_out_conv[l], lam_init, subln_g[l], w_out_attn[l], w_o[l],
                  norm_mix_g[l], norm_ffn_g[l], w_gate[l], w_up[l], w_down[l])
        xp, kp, vp, cp = trunk_layer(xp, zero_buf, functools.partial(diff_attn_prompt, lam=lam), *shared)
        attend_s = functools.partial(diff_attn_sample, cache_k=cache_k, cache_v=cache_v,
                                     page_table=page_table, layer=l, lam=lam)
        xs, ksn, vsn, csn = trunk_layer(xs, state_conv[l], attend_s, *shared)
        kp_l.append(kp); vp_l.append(vp); cp_l.append(cp)
        ks_l.append(ksn); vs_l.append(vsn); cs_l.append(csn)
    y_prompt = rmsnorm(xp, norm_final_g)
    y_sample = rmsnorm(xs, norm_final_g)
    return (y_prompt, y_sample, jnp.stack(kp_l), jnp.stack(vp_l), jnp.stack(cp_l),
            jnp.stack(ks_l), jnp.stack(vs_l), jnp.stack(cs_l))
```

```python
import functools
import math

import jax
import jax.numpy as jnp
from jax import lax
from jax.experimental import pallas as pl
from jax.experimental.pallas import tpu as pltpu

F32 = jnp.float32
BF16 = jnp.bfloat16
EPS = 1e-6
MASK_VALUE = -1e30
CONV_WIDTH = 3
V7X_VMEM_LIMIT_BYTES = 56 * 1024 * 1024
LANES = 128


def _pick(n, pref):
    t = pref
    while t >= LANES:
        if n % t == 0:
            return t
        t //= 2
    return n


def _params(*sem):
    return pltpu.CompilerParams(dimension_semantics=sem, vmem_limit_bytes=V7X_VMEM_LIMIT_BYTES)


def _rms(x, g):
    return x * lax.rsqrt(jnp.mean(x * x, axis=-1, keepdims=True) + EPS) * g


def _dot(a, b):
    return jnp.dot(a, b, preferred_element_type=F32)


def _dot_nt(a, b):
    return lax.dot_general(a, b, (((1,), (1,)), ((), ())), preferred_element_type=F32)


def _norm_kernel(x_ref, g_ref, o_ref):
    o_ref[...] = _rms(x_ref[...], g_ref[...]).astype(o_ref.dtype)


def _norm_cast(x, g):
    m, d = x.shape
    tm = _pick(m, 512)
    return pl.pallas_call(
        _norm_kernel,
        out_shape=jax.ShapeDtypeStruct((m, d), BF16),
        grid=(m // tm,),
        in_specs=[pl.BlockSpec((tm, d), lambda i: (i, 0)),
                  pl.BlockSpec((1, d), lambda i: (0, 0))],
        out_specs=pl.BlockSpec((tm, d), lambda i: (i, 0)),
        compiler_params=_params("parallel"),
        name="norm_cast",
    )(x, g.reshape(1, d))


def _conv_taps(u, um1, um2, cw_ref):
    return um2 * cw_ref[0:1, :] + um1 * cw_ref[1:2, :] + u * cw_ref[2:3, :]


def _conv_prompt_kernel(hn_ref, wb_ref, wc_ref, wh_ref, cw_ref, ya_ref, tail_ref, carry_ref, *, tiles_per_seq):
    i = pl.program_id(1)

    @pl.when(i % tiles_per_seq == 0)
    def _():
        carry_ref[...] = jnp.zeros_like(carry_ref)

    hn = hn_ref[...]
    u = _dot(hn, wc_ref[...]) * _dot(hn, wh_ref[...])
    tm = u.shape[0]
    row = lax.broadcasted_iota(jnp.int32, u.shape, 0)
    c0 = carry_ref[0:1, :]
    c1 = carry_ref[1:2, :]
    um1 = jnp.where(row == 0, c1, pltpu.roll(u, 1, 0))
    um2 = jnp.where(row == 0, c0, jnp.where(row == 1, c1, pltpu.roll(u, 2, 0)))
    yc = _conv_taps(u, um1, um2, cw_ref)
    ya_ref[...] = (_dot(hn, wb_ref[...]) * yc).astype(ya_ref.dtype)
    last = u[tm - (CONV_WIDTH - 1):, :]
    carry_ref[...] = last
    tail_ref[0] = last


def _conv_prompt(hn, w_in, cw_t, batch, seq):
    m, d = hn.shape
    dc = cw_t.shape[1]
    tm = _pick(seq, 1024)
    tn = _pick(dc, 512)
    nseg = dc // tn
    kern = functools.partial(_conv_prompt_kernel, tiles_per_seq=seq // tm)
    return pl.pallas_call(
        kern,
        out_shape=(jax.ShapeDtypeStruct((m, dc), BF16),
                   jax.ShapeDtypeStruct((batch, CONV_WIDTH - 1, dc), F32)),
        grid=(nseg, m // tm),
        in_specs=[pl.BlockSpec((tm, d), lambda j, i: (i, 0)),
                  pl.BlockSpec((d, tn), lambda j, i: (0, j)),
                  pl.BlockSpec((d, tn), lambda j, i: (0, nseg + j)),
                  pl.BlockSpec((d, tn), lambda j, i: (0, 2 * nseg + j)),
                  pl.BlockSpec((CONV_WIDTH, tn), lambda j, i: (0, j))],
        out_specs=(pl.BlockSpec((tm, tn), lambda j, i: (i, j)),
                   pl.BlockSpec((1, CONV_WIDTH - 1, tn), lambda j, i: (i // (seq // tm), 0, j))),
        scratch_shapes=[pltpu.VMEM((CONV_WIDTH - 1, tn), F32)],
        compiler_params=_params("parallel", "arbitrary"),
        name="conv_inproj_prompt",
    )(hn, w_in, w_in, w_in, cw_t)


def _conv_sample_kernel(hn_ref, wb_ref, wc_ref, wh_ref, cw_ref, um1_ref, um2_ref, ya_ref, u_ref):
    hn = hn_ref[...]
    u = _dot(hn, wc_ref[...]) * _dot(hn, wh_ref[...])
    yc = _conv_taps(u, um1_ref[...], um2_ref[...], cw_ref)
    ya_ref[...] = (_dot(hn, wb_ref[...]) * yc).astype(ya_ref.dtype)
    u_ref[...] = u


def _conv_sample(hn, w_in, cw_t, um1, um2):
    m, d = hn.shape
    dc = cw_t.shape[1]
    tn = _pick(dc, 512)
    nseg = dc // tn
    return pl.pallas_call(
        _conv_sample_kernel,
        out_shape=(jax.ShapeDtypeStruct((m, dc), BF16),
                   jax.ShapeDtypeStruct((m, dc), F32)),
        grid=(nseg,),
        in_specs=[pl.BlockSpec((m, d), lambda j: (0, 0)),
                  pl.BlockSpec((d, tn), lambda j: (0, j)),
                  pl.BlockSpec((d, tn), lambda j: (0, nseg + j)),
                  pl.BlockSpec((d, tn), lambda j: (0, 2 * nseg + j)),
                  pl.BlockSpec((CONV_WIDTH, tn), lambda j: (0, j)),
                  pl.BlockSpec((m, tn), lambda j: (0, j)),
                  pl.BlockSpec((m, tn), lambda j: (0, j))],
        out_specs=(pl.BlockSpec((m, tn), lambda j: (0, j)),
                   pl.BlockSpec((m, tn), lambda j: (0, j))),
        compiler_params=_params("parallel"),
        name="conv_inproj_sample",
    )(hn, w_in, w_in, w_in, cw_t, um1, um2)


def _qkv_kernel(hn_ref, wq_ref, wk_ref, wv_ref, q_ref, k_ref, v_ref, kb_ref, vb_ref, *, scale):
    hn = hn_ref[...]
    q_ref[...] = (_dot(hn, wq_ref[...]) * scale).astype(q_ref.dtype)
    k = _dot(hn, wk_ref[...])
    k_ref[...] = k
    kb_ref[...] = k.astype(kb_ref.dtype)
    v = _dot(hn, wv_ref[...])
    v_ref[...] = v
    vb_ref[...] = v.astype(vb_ref.dtype)


def _qkv_proj(hn, w_in, col0, da, scale, q_dtype):
    m, d = hn.shape
    tm = _pick(m, 1024)
    tn = _pick(da, 512)
    nseg = da // tn
    c0 = col0 // tn
    blk = pl.BlockSpec((tm, tn), lambda j, i: (i, j))
    return pl.pallas_call(
        functools.partial(_qkv_kernel, scale=scale),
        out_shape=(jax.ShapeDtypeStruct((m, da), q_dtype),
                   jax.ShapeDtypeStruct((m, da), F32),
                   jax.ShapeDtypeStruct((m, da), F32),
                   jax.ShapeDtypeStruct((m, da), BF16),
                   jax.ShapeDtypeStruct((m, da), BF16)),
        grid=(nseg, m // tm),
        in_specs=[pl.BlockSpec((tm, d), lambda j, i: (i, 0)),
                  pl.BlockSpec((d, tn), lambda j, i: (0, c0 + j)),
                  pl.BlockSpec((d, tn), lambda j, i: (0, c0 + nseg + j)),
                  pl.BlockSpec((d, tn), lambda j, i: (0, c0 + 2 * nseg + j))],
        out_specs=(blk, blk, blk, blk, blk),
        compiler_params=_params("parallel", "parallel"),
        name="qkv_proj",
    )(hn, w_in, w_in, w_in)


def _diff_lambda(lq1_ref, lk1_ref, lq2_ref, lk2_ref, lam_init):
    a = jnp.sum(lq1_ref[...] * lk1_ref[...], axis=-1, keepdims=True)
    b = jnp.sum(lq2_ref[...] * lk2_ref[...], axis=-1, keepdims=True)
    return jnp.exp(a) - jnp.exp(b) + lam_init


def _attn_prompt_kernel(lq1_ref, lk1_ref, lq2_ref, lk2_ref, g_ref, q_ref, k_ref, v_ref, o_ref,
                        acc1_ref, acc2_ref, *, tq, tk, lam_init):
    seq = q_ref.shape[0]
    hd = q_ref.shape[1] // 2
    lam = _diff_lambda(lq1_ref, lk1_ref, lq2_ref, lk2_ref, lam_init)

    def softmax_step(s, m, l, acc_ref, v):
        m_new = jnp.maximum(m, jnp.max(s, axis=-1, keepdims=True))
        alpha = jnp.exp(m - m_new)
        p = jnp.exp(s - m_new)
        l_new = alpha * l + jnp.sum(p, axis=-1, keepdims=True)
        acc_ref[...] = alpha * acc_ref[...] + _dot(p.astype(v.dtype), v)
        return m_new, l_new

    for qi in range(seq // tq):
        q = q_ref[qi * tq:(qi + 1) * tq, :]
        q1 = q[:, :hd]
        q2 = q[:, hd:]

        def block(kstart, carry, masked):
            m1, l1, m2, l2 = carry
            k = k_ref[pl.ds(kstart, tk), :]
            v = v_ref[pl.ds(kstart, tk), :]
            s1 = _dot_nt(q1, k[:, :hd])
            s2 = _dot_nt(q2, k[:, hd:])
            if masked:
                keep = (lax.broadcasted_iota(jnp.int32, s1.shape, 1)
                        <= lax.broadcasted_iota(jnp.int32, s1.shape, 0))
                s1 = jnp.where(keep, s1, MASK_VALUE)
                s2 = jnp.where(keep, s2, MASK_VALUE)
            m1, l1 = softmax_step(s1, m1, l1, acc1_ref, v)
            m2, l2 = softmax_step(s2, m2, l2, acc2_ref, v)
            return m1, l1, m2, l2

        acc1_ref[...] = jnp.zeros_like(acc1_ref)
        acc2_ref[...] = jnp.zeros_like(acc2_ref)
        neg = jnp.full((tq, 1), -jnp.inf, F32)
        zero = jnp.zeros((tq, 1), F32)
        carry = block(qi * tk, (neg, zero, neg, zero), True)
        if qi > 0:
            carry = lax.fori_loop(
                0, qi, lambda ki, c: block(pl.multiple_of(ki * tk, tk), c, False), carry)
        _, l1, _, l2 = carry
        o = acc1_ref[...] / l1 - lam * (acc2_ref[...] / l2)
        o_ref[qi * tq:(qi + 1) * tq, :] = (_rms(o, g_ref[...]) * (1.0 - lam_init)).astype(o_ref.dtype)


def _attn_prompt(q, kb, vb, lam_refs, subln_g, batch, seq, n_heads, lam_init):
    m, da = q.shape
    hd2 = da // n_heads
    tq = _pick(seq, 512)
    vec = lambda n: pl.BlockSpec((1, n), lambda b, h: (0, 0))
    head = pl.BlockSpec((seq, hd2), lambda b, h: (b, h))
    return pl.pallas_call(
        functools.partial(_attn_prompt_kernel, tq=tq, tk=tq, lam_init=lam_init),
        out_shape=jax.ShapeDtypeStruct((m, da), BF16),
        grid=(batch, n_heads),
        in_specs=[vec(hd2 // 2)] * 4 + [vec(hd2), head, head, head],
        out_specs=head,
        scratch_shapes=[pltpu.VMEM((tq, hd2), F32), pltpu.VMEM((tq, hd2), F32)],
        compiler_params=_params("parallel", "parallel"),
        name="diff_attn_prompt",
    )(*lam_refs, subln_g, q, kb, vb)


def _attn_sample_kernel(pt_ref, lq1_ref, lk1_ref, lq2_ref, lk2_ref, g_ref, q_ref, kn_ref, vn_ref, *rest,
                        pages_per_step, lam_init):
    del pt_ref
    k_refs = rest[:pages_per_step]
    v_refs = rest[pages_per_step:2 * pages_per_step]
    o_ref, m1_ref, l1_ref, m2_ref, l2_ref, acc1_ref, acc2_ref = rest[2 * pages_per_step:]
    j = pl.program_id(1)
    hd = q_ref.shape[-1] // 2
    q = q_ref[0]

    @pl.when(j == 0)
    def _():
        sn = q * kn_ref[0]
        m1_ref[...] = jnp.sum(sn[:, :hd], axis=-1, keepdims=True)
        m2_ref[...] = jnp.sum(sn[:, hd:], axis=-1, keepdims=True)
        l1_ref[...] = jnp.ones_like(l1_ref)
        l2_ref[...] = jnp.ones_like(l2_ref)
        acc1_ref[...] = vn_ref[0]
        acc2_ref[...] = vn_ref[0]

    def update(s, v, m_ref, l_ref, acc_ref):
        m = m_ref[...]
        m_new = jnp.maximum(m, jnp.max(s, axis=0))
        alpha = jnp.exp(m - m_new)
        p = jnp.exp(s - m_new[None])
        l_ref[...] = alpha * l_ref[...] + jnp.sum(p, axis=0)
        acc_ref[...] = alpha * acc_ref[...] + jnp.sum(p * v, axis=0)
        m_ref[...] = m_new

    for k_ref, v_ref in zip(k_refs, v_refs):
        prod = k_ref[...] * q[None]
        v = v_ref[...]
        update(jnp.sum(prod[:, :, :hd], axis=-1, keepdims=True), v, m1_ref, l1_ref, acc1_ref)
        update(jnp.sum(prod[:, :, hd:], axis=-1, keepdims=True), v, m2_ref, l2_ref, acc2_ref)

    @pl.when(j == pl.num_programs(1) - 1)
    def _():
        lam = _diff_lambda(lq1_ref, lk1_ref, lq2_ref, lk2_ref, lam_init)
        o = acc1_ref[...] / l1_ref[...] - lam * (acc2_ref[...] / l2_ref[...])
        o_ref[0] = (_rms(o, g_ref[...]) * (1.0 - lam_init)).astype(o_ref.dtype)


def _attn_sample(q, k_new, v_new, cache_k, cache_v, page_table, layer, lam_refs, subln_g, lam_init):
    db, n_heads, hd2 = q.shape
    page = cache_k.shape[2]
    n_pages = page_table.shape[1]
    pps = 4 if n_pages % 4 == 0 else 1
    vec = lambda n: pl.BlockSpec((1, n), lambda b, j, pt: (0, 0))
    tok = pl.BlockSpec((1, n_heads, hd2), lambda b, j, pt: (b, 0, 0))

    def page_spec(p):
        return pl.BlockSpec((None, None, page, n_heads, hd2),
                            lambda b, j, pt: (layer, pt[b, j * pps + p], 0, 0, 0))

    stat = pltpu.VMEM((n_heads, 1), F32)
    acc = pltpu.VMEM((n_heads, hd2), F32)
    grid_spec = pltpu.PrefetchScalarGridSpec(
        num_scalar_prefetch=1,
        grid=(db, n_pages // pps),
        in_specs=[vec(hd2 // 2)] * 4 + [vec(hd2), tok, tok, tok]
                 + [page_spec(p) for p in range(pps)] * 2,
        out_specs=tok,
        scratch_shapes=[stat, stat, stat, stat, acc, acc],
    )
    return pl.pallas_call(
        functools.partial(_attn_sample_kernel, pages_per_step=pps, lam_init=lam_init),
        out_shape=jax.ShapeDtypeStruct((db, n_heads, hd2), BF16),
        grid_spec=grid_spec,
        compiler_params=_params("parallel", "arbitrary"),
        name="diff_attn_sample",
    )(page_table, *lam_refs, subln_g, q, k_new, v_new, *([cache_k] * pps), *([cache_v] * pps))


def _merge_kernel(ya_ref, ob_ref, hn_ref, wc_ref, wa_ref, wga_ref, wgb_ref, m_ref):
    hn = hn_ref[...]
    y_a = _dot(ya_ref[...], wc_ref[...])
    y_b = _dot(ob_ref[...], wa_ref[...])
    g_a = jax.nn.sigmoid(_dot(hn, wga_ref[...]))
    g_b = jax.nn.sigmoid(_dot(hn, wgb_ref[...]))
    m_ref[...] = (g_a * y_a + g_b * y_b).astype(m_ref.dtype)


def _merge(ya, ob, hn, w_out_conv, w_out_attn, w_in, gate_col0):
    m, d = hn.shape
    dm = w_out_conv.shape[1]
    tm = _pick(m, 1024)
    tn = _pick(dm, 512)
    nseg = dm // tn
    g0 = gate_col0 // tn
    row = lambda k: pl.BlockSpec((tm, k), lambda j, i: (i, 0))
    return pl.pallas_call(
        _merge_kernel,
        out_shape=jax.ShapeDtypeStruct((m, dm), BF16),
        grid=(nseg, m // tm),
        in_specs=[row(ya.shape[1]), row(ob.shape[1]), row(d),
                  pl.BlockSpec((ya.shape[1], tn), lambda j, i: (0, j)),
                  pl.BlockSpec((ob.shape[1], tn), lambda j, i: (0, j)),
                  pl.BlockSpec((d, tn), lambda j, i: (0, g0 + j)),
                  pl.BlockSpec((d, tn), lambda j, i: (0, g0 + nseg + j))],
        out_specs=pl.BlockSpec((tm, tn), lambda j, i: (i, j)),
        compiler_params=_params("parallel", "parallel"),
        name="gated_merge",
    )(ya, ob, hn, w_out_conv, w_out_attn, w_in, w_in)


def _oproj_kernel(m_ref, wo_ref, x_ref, g_ref, h_ref, f_ref):
    h = x_ref[...] + _dot(m_ref[...], wo_ref[...])
    h_ref[...] = h
    f_ref[...] = _rms(h, g_ref[...]).astype(f_ref.dtype)


def _oproj_norm(mg, w_o, x, g):
    m, d = x.shape
    tm = _pick(m, 512)
    return pl.pallas_call(
        _oproj_kernel,
        out_shape=(jax.ShapeDtypeStruct((m, d), F32), jax.ShapeDtypeStruct((m, d), BF16)),
        grid=(m // tm,),
        in_specs=[pl.BlockSpec((tm, mg.shape[1]), lambda i: (i, 0)),
                  pl.BlockSpec(w_o.shape, lambda i: (0, 0)),
                  pl.BlockSpec((tm, d), lambda i: (i, 0)),
                  pl.BlockSpec((1, d), lambda i: (0, 0))],
        out_specs=(pl.BlockSpec((tm, d), lambda i: (i, 0)), pl.BlockSpec((tm, d), lambda i: (i, 0))),
        compiler_params=_params("parallel"),
        name="oproj_norm",
    )(mg, w_o, x, g.reshape(1, d))


def _ffn_kernel(f_ref, h_ref, wg_ref, wu_ref, wd_ref, g_ref, y_ref, acc_ref, *, final_norm):
    c = pl.program_id(1)

    @pl.when(c == 0)
    def _():
        acc_ref[...] = h_ref[...]

    f = f_ref[...]
    a = jax.nn.silu(_dot(f, wg_ref[...])) * _dot(f, wu_ref[...])
    acc_ref[...] += _dot(a.astype(wd_ref.dtype), wd_ref[...])

    @pl.when(c == pl.num_programs(1) - 1)
    def _():
        y_ref[...] = _rms(acc_ref[...], g_ref[...]) if final_norm else acc_ref[...]


def _ffn(f, h, w_gate, w_up, w_down, g_final, final_norm):
    m, d = h.shape
    dff = w_gate.shape[1]
    tm = _pick(m, 512)
    tc = _pick(dff, 512)
    return pl.pallas_call(
        functools.partial(_ffn_kernel, final_norm=final_norm),
        out_shape=jax.ShapeDtypeStruct((m, d), F32),
        grid=(m // tm, dff // tc),
        in_specs=[pl.BlockSpec((tm, d), lambda i, c: (i, 0)),
                  pl.BlockSpec((tm, d), lambda i, c: (i, 0)),
                  pl.BlockSpec((d, tc), lambda i, c: (0, c)),
                  pl.BlockSpec((d, tc), lambda i, c: (0, c)),
                  pl.BlockSpec((tc, d), lambda i, c: (c, 0)),
                  pl.BlockSpec((1, d), lambda i, c: (0, 0))],
        out_specs=pl.BlockSpec((tm, d), lambda i, c: (i, 0)),
        scratch_shapes=[pltpu.VMEM((tm, d), F32)],
        compiler_params=_params("parallel", "arbitrary"),
        name="ffn_final",
    )(f, h, w_gate, w_up, w_down, g_final.reshape(1, d))


def kernel(x_prompt, x_sample, cache_k, cache_v, state_conv, page_table, w_in, conv_w, w_out_conv, lambda_q1, lambda_k1, lambda_q2, lambda_k2, subln_g, w_out_attn, w_o, norm_mix_g, norm_ffn_g, w_gate, w_up, w_down, norm_final_g):
    batch, seq, d = x_prompt.shape
    db, dseq, _ = x_sample.shape
    assert dseq == 1, "decode rows carry one new token per sequence"
    depth = w_in.shape[0]
    n_heads, hd2 = cache_k.shape[-2:]
    hd = hd2 // 2
    da = n_heads * hd2
    dc = conv_w.shape[1]
    assert state_conv.shape[2] == CONV_WIDTH - 1 and conv_w.shape[2] == CONV_WIDTH
    scale = hd ** -0.5
    qkv_col0 = 3 * dc
    gate_col0 = 3 * dc + 3 * da

    xp = x_prompt.reshape(batch * seq, d)
    xs = x_sample.reshape(db, d)
    outs = [[] for _ in range(6)]
    for l in range(depth):
        lam_init = 0.8 - 0.6 * math.exp(-0.3 * l)
        w_in_b = w_in[l].astype(BF16)
        w_oc_b = w_out_conv[l].astype(BF16)
        w_oa_b = w_out_attn[l].astype(BF16)
        w_o_b = w_o[l].astype(BF16)
        w_g_b = w_gate[l].astype(BF16)
        w_u_b = w_up[l].astype(BF16)
        w_d_b = w_down[l].astype(BF16)
        cw_t = conv_w[l].T
        lam_refs = [a[l].reshape(1, hd) for a in (lambda_q1, lambda_k1, lambda_q2, lambda_k2)]
        sub_g = subln_g[l].reshape(1, hd2)

        hn = _norm_cast(xp, norm_mix_g[l])
        ya, conv_p = _conv_prompt(hn, w_in_b, cw_t, batch, seq)
        q, k32, v32, kb, vb = _qkv_proj(hn, w_in_b, qkv_col0, da, scale, BF16)
        ob = _attn_prompt(q, kb, vb, lam_refs, sub_g, batch, seq, n_heads, lam_init)
        mg = _merge(ya, ob, hn, w_oc_b, w_oa_b, w_in_b, gate_col0)
        h, f = _oproj_norm(mg, w_o_b, xp, norm_ffn_g[l])
        xp = _ffn(f, h, w_g_b, w_u_b, w_d_b, norm_final_g, l == depth - 1)
        outs[0].append(k32.reshape(batch, seq, n_heads, hd2))
        outs[1].append(v32.reshape(batch, seq, n_heads, hd2))
        outs[2].append(conv_p)

        hs = _norm_cast(xs, norm_mix_g[l])
        st = state_conv[l]
        ya_s, u_s = _conv_sample(hs, w_in_b, cw_t, st[:, 1, :], st[:, 0, :])
        q_s, k_s, v_s, _, _ = _qkv_proj(hs, w_in_b, qkv_col0, da, scale, F32)
        to_heads = lambda a: a.reshape(db, n_heads, hd2)
        ob_s = _attn_sample(to_heads(q_s), to_heads(k_s), to_heads(v_s), cache_k, cache_v, page_table, l,
                            lam_refs, sub_g, lam_init)
        mg_s = _merge(ya_s, ob_s.reshape(db, da), hs, w_oc_b, w_oa_b, w_in_b, gate_col0)
        h_s, f_s = _oproj_norm(mg_s, w_o_b, xs, norm_ffn_g[l])
        xs = _ffn(f_s, h_s, w_g_b, w_u_b, w_d_b, norm_final_g, l == depth - 1)
        outs[3].append(k_s.reshape(db, 1, n_heads, hd2))
        outs[4].append(v_s.reshape(db, 1, n_heads, hd2))
        outs[5].append(jnp.stack([st[:, 1, :], u_s], axis=1))

    y_prompt = xp.reshape(batch, seq, d)
    y_sample = xs.reshape(db, 1, d)
    return (y_prompt, y_sample) + tuple(jnp.stack(o) for o in outs)
```

```python
import functools
import math

import jax
import jax.numpy as jnp
from jax import lax
from jax.experimental import pallas as pl
from jax.experimental.pallas import tpu as pltpu

F32 = jnp.float32
BF16 = jnp.bfloat16
EPS = 1e-6
MASK_VALUE = -1e30
CONV_WIDTH = 3
LOG2_E = math.log2(math.e)
V7X_VMEM_LIMIT_BYTES = 56 * 1024 * 1024
LANES = 128


def _pick(n, pref):
    t = pref
    while t >= LANES:
        if n % t == 0:
            return t
        t //= 2
    return n


def _params(*sem):
    return pltpu.CompilerParams(dimension_semantics=sem, vmem_limit_bytes=V7X_VMEM_LIMIT_BYTES)


def _rms(x, g):
    return x * lax.rsqrt(jnp.mean(x * x, axis=-1, keepdims=True) + EPS) * g


def _dot(a, b):
    return jnp.dot(a, b, preferred_element_type=F32)


def _dot_nt(a, b):
    return lax.dot_general(a, b, (((1,), (1,)), ((), ())), preferred_element_type=F32)


def _resident(shape, index_map):
    return pl.BlockSpec(shape, index_map, pipeline_mode=pl.Buffered(1))


def _cast_weights(first_step, pairs):
    @pl.when(first_step)
    def _():
        for src, dst in pairs:
            dst[...] = src[...].astype(dst.dtype)


def _norm_kernel(x_ref, xs_ref, g_ref, o_ref, os_ref):
    o_ref[...] = _rms(x_ref[...], g_ref[...]).astype(o_ref.dtype)

    @pl.when(pl.program_id(0) == 0)
    def _():
        os_ref[...] = _rms(xs_ref[...], g_ref[...]).astype(os_ref.dtype)


def _norm_cast(x, xs, g):
    m, d = x.shape
    ms = xs.shape[0]
    tm = _pick(m, 512)
    return pl.pallas_call(
        _norm_kernel,
        out_shape=(jax.ShapeDtypeStruct((m, d), BF16), jax.ShapeDtypeStruct((ms, d), BF16)),
        grid=(m // tm,),
        in_specs=[pl.BlockSpec((tm, d), lambda i: (i, 0)),
                  pl.BlockSpec((ms, d), lambda i: (0, 0)),
                  pl.BlockSpec((1, d), lambda i: (0, 0))],
        out_specs=(pl.BlockSpec((tm, d), lambda i: (i, 0)),
                   pl.BlockSpec((ms, d), lambda i: (0, 0))),
        compiler_params=_params("arbitrary"),
        name="norm_cast",
    )(x, xs, g.reshape(1, d))


def _conv_taps(u, um1, um2, cw_ref):
    return um2 * cw_ref[0:1, :] + um1 * cw_ref[1:2, :] + u * cw_ref[2:3, :]


def _conv_kernel(hn_ref, hs_ref, um1_ref, um2_ref, wb_ref, wc_ref, wh_ref, cw_ref,
                 ya_ref, tail_ref, yas_ref, us_ref, wbb, wcb, whb, carry_ref, *, tiles_per_seq):
    i = pl.program_id(1)
    _cast_weights(i == 0, ((wb_ref, wbb), (wc_ref, wcb), (wh_ref, whb)))

    @pl.when(i == 0)
    def _():
        hs = hs_ref[...]
        us = _dot(hs, wcb[...]) * _dot(hs, whb[...])
        yas_ref[...] = (_dot(hs, wbb[...]) * _conv_taps(us, um1_ref[...], um2_ref[...], cw_ref)).astype(yas_ref.dtype)
        us_ref[...] = us

    @pl.when(i % tiles_per_seq == 0)
    def _():
        carry_ref[...] = jnp.zeros_like(carry_ref)

    hn = hn_ref[...]
    u = _dot(hn, wcb[...]) * _dot(hn, whb[...])
    tm = u.shape[0]
    row = lax.broadcasted_iota(jnp.int32, u.shape, 0)
    c0 = carry_ref[0:1, :]
    c1 = carry_ref[1:2, :]
    um1 = jnp.where(row == 0, c1, pltpu.roll(u, 1, 0))
    um2 = jnp.where(row == 0, c0, jnp.where(row == 1, c1, pltpu.roll(u, 2, 0)))
    yc = _conv_taps(u, um1, um2, cw_ref)
    ya_ref[...] = (_dot(hn, wbb[...]) * yc).astype(ya_ref.dtype)
    last = u[tm - (CONV_WIDTH - 1):, :]
    carry_ref[...] = last
    tail_ref[0] = last


def _conv_inproj(hn, hs, um1, um2, w_in, cw_t, batch, seq):
    m, d = hn.shape
    ms = hs.shape[0]
    dc = cw_t.shape[1]
    tm = _pick(seq, 1024)
    tn = _pick(dc, 512)
    nseg = dc // tn
    wspec = lambda off: _resident((d, tn), lambda j, i: (0, off + j))
    srow = pl.BlockSpec((ms, tn), lambda j, i: (0, j))
    return pl.pallas_call(
        functools.partial(_conv_kernel, tiles_per_seq=seq // tm),
        out_shape=(jax.ShapeDtypeStruct((m, dc), BF16),
                   jax.ShapeDtypeStruct((batch, CONV_WIDTH - 1, dc), F32),
                   jax.ShapeDtypeStruct((ms, dc), BF16),
                   jax.ShapeDtypeStruct((ms, dc), F32)),
        grid=(nseg, m // tm),
        in_specs=[pl.BlockSpec((tm, d), lambda j, i: (i, 0)),
                  pl.BlockSpec((ms, d), lambda j, i: (0, 0)),
                  srow, srow,
                  wspec(0), wspec(nseg), wspec(2 * nseg),
                  pl.BlockSpec((CONV_WIDTH, tn), lambda j, i: (0, j))],
        out_specs=(pl.BlockSpec((tm, tn), lambda j, i: (i, j)),
                   pl.BlockSpec((1, CONV_WIDTH - 1, tn), lambda j, i: (i // (seq // tm), 0, j)),
                   srow, srow),
        scratch_shapes=[pltpu.VMEM((d, tn), BF16)] * 3 + [pltpu.VMEM((CONV_WIDTH - 1, tn), F32)],
        compiler_params=_params("arbitrary", "arbitrary"),
        name="conv_inproj",
    )(hn, hs, um1, um2, w_in, w_in, w_in, cw_t)


def _qkv_kernel(hn_ref, hs_ref, wq_ref, wk_ref, wv_ref, q_ref, k_ref, v_ref, kb_ref, vb_ref,
                qs_ref, ks_ref, vs_ref, wqb, wkb, wvb, *, q_scale, qs_scale):
    i = pl.program_id(1)
    _cast_weights(i == 0, ((wq_ref, wqb), (wk_ref, wkb), (wv_ref, wvb)))

    @pl.when(i == 0)
    def _():
        hs = hs_ref[...]
        qs_ref[...] = _dot(hs, wqb[...]) * qs_scale
        ks_ref[...] = _dot(hs, wkb[...])
        vs_ref[...] = _dot(hs, wvb[...])

    hn = hn_ref[...]
    q_ref[...] = (_dot(hn, wqb[...]) * q_scale).astype(q_ref.dtype)
    k = _dot(hn, wkb[...])
    k_ref[...] = k
    kb_ref[...] = k.astype(kb_ref.dtype)
    v = _dot(hn, wvb[...])
    v_ref[...] = v
    vb_ref[...] = v.astype(vb_ref.dtype)


def _qkv_proj(hn, hs, w_in, col0, da, q_scale, qs_scale):
    m, d = hn.shape
    ms = hs.shape[0]
    tm = _pick(m, 1024)
    tn = _pick(da, 512)
    nseg = da // tn
    c0 = col0 // tn
    blk = pl.BlockSpec((tm, tn), lambda j, i: (i, j))
    srow = pl.BlockSpec((ms, tn), lambda j, i: (0, j))
    wspec = lambda off: _resident((d, tn), lambda j, i: (0, c0 + off + j))
    return pl.pallas_call(
        functools.partial(_qkv_kernel, q_scale=q_scale, qs_scale=qs_scale),
        out_shape=(jax.ShapeDtypeStruct((m, da), BF16),
                   jax.ShapeDtypeStruct((m, da), F32),
                   jax.ShapeDtypeStruct((m, da), F32),
                   jax.ShapeDtypeStruct((m, da), BF16),
                   jax.ShapeDtypeStruct((m, da), BF16),
                   jax.ShapeDtypeStruct((ms, da), F32),
                   jax.ShapeDtypeStruct((ms, da), F32),
                   jax.ShapeDtypeStruct((ms, da), F32)),
        grid=(nseg, m // tm),
        in_specs=[pl.BlockSpec((tm, d), lambda j, i: (i, 0)),
                  pl.BlockSpec((ms, d), lambda j, i: (0, 0)),
                  wspec(0), wspec(nseg), wspec(2 * nseg)],
        out_specs=(blk, blk, blk, blk, blk, srow, srow, srow),
        scratch_shapes=[pltpu.VMEM((d, tn), BF16)] * 3,
        compiler_params=_params("arbitrary", "arbitrary"),
        name="qkv_proj",
    )(hn, hs, w_in, w_in, w_in)


def _diff_lambda(lq1_ref, lk1_ref, lq2_ref, lk2_ref, lam_init):
    a = jnp.sum(lq1_ref[...] * lk1_ref[...], axis=-1, keepdims=True)
    b = jnp.sum(lq2_ref[...] * lk2_ref[...], axis=-1, keepdims=True)
    return jnp.exp(a) - jnp.exp(b) + lam_init


def _attn_prompt_kernel(lq1_ref, lk1_ref, lq2_ref, lk2_ref, g_ref, q_ref, k_ref, v_ref, o_ref,
                        s_ref, p_ref, c_ref, *, tq, rc, lam_init):
    seq = q_ref.shape[0]
    hd = q_ref.shape[1] // 2
    nq = seq // tq
    lam = _diff_lambda(lq1_ref, lk1_ref, lq2_ref, lk2_ref, lam_init)

    def scores(qi):
        slot = qi % 2
        lo, hi = qi * tq, (qi + 1) * tq
        q = q_ref[lo:hi, :]
        for mp in range(2):
            cols = slice(mp * hd, (mp + 1) * hd)
            if qi > 0:
                s_ref[slot, mp, :, :lo] = _dot_nt(q[:, cols], k_ref[:lo, cols])
            sd = _dot_nt(q[:, cols], k_ref[lo:hi, cols])
            keep = (lax.broadcasted_iota(jnp.int32, sd.shape, 1) <= lax.broadcasted_iota(jnp.int32, sd.shape, 0))
            s_ref[slot, mp, :, lo:hi] = jnp.where(keep, sd, MASK_VALUE)

    def softmax_diff(qi):
        slot = qi % 2
        hi = (qi + 1) * tq
        for r in range(tq // rc):
            rows = slice(r * rc, (r + 1) * rc)
            x1 = s_ref[slot, 0, rows, :hi]
            x2 = s_ref[slot, 1, rows, :hi]
            e1 = jnp.exp2(x1 - jnp.max(x1, axis=-1, keepdims=True))
            e2 = jnp.exp2(x2 - jnp.max(x2, axis=-1, keepdims=True))
            l1 = jnp.sum(e1, axis=-1, keepdims=True)
            l2 = jnp.sum(e2, axis=-1, keepdims=True)
            p_ref[slot, rows, :hi] = (e1 - e2 * (lam * l1 / l2)).astype(p_ref.dtype)
            c_ref[slot, rows, :] = 1.0 / l1

    def values(qi):
        slot = qi % 2
        lo, hi = qi * tq, (qi + 1) * tq
        o = _dot(p_ref[slot, :, :hi], v_ref[:hi, :]) * c_ref[slot]
        o_ref[lo:hi, :] = (_rms(o, g_ref[...]) * (1.0 - lam_init)).astype(o_ref.dtype)

    scores(0)
    for qi in range(nq):
        if qi + 1 < nq:
            scores(qi + 1)
        softmax_diff(qi)
        values(qi)


def _attn_prompt(q, kb, vb, lam_refs, subln_g, batch, seq, n_heads, lam_init):
    m, da = q.shape
    hd2 = da // n_heads
    tq = _pick(seq, 256)
    rc = 32
    vec = lambda n: pl.BlockSpec((1, n), lambda b, h: (0, 0))
    head = pl.BlockSpec((seq, hd2), lambda b, h: (b, h))
    return pl.pallas_call(
        functools.partial(_attn_prompt_kernel, tq=tq, rc=rc, lam_init=lam_init),
        out_shape=jax.ShapeDtypeStruct((m, da), BF16),
        grid=(batch, n_heads),
        in_specs=[vec(hd2 // 2)] * 4 + [vec(hd2), head, head, head],
        out_specs=head,
        scratch_shapes=[pltpu.VMEM((2, 2, tq, seq), F32), pltpu.VMEM((2, tq, seq), BF16),
                        pltpu.VMEM((2, tq, 1), F32)],
        compiler_params=_params("parallel", "parallel"),
        name="diff_attn_prompt",
    )(*lam_refs, subln_g, q, kb, vb)


def _attn_sample_kernel(pt_ref, lq1_ref, lk1_ref, lq2_ref, lk2_ref, g_ref, q2_ref, kn_ref, vn_ref, *rest,
                        pages_per_step, lam_init):
    del pt_ref
    k_refs = rest[:pages_per_step]
    v_refs = rest[pages_per_step:2 * pages_per_step]
    o_ref, m_ref, l_ref, acc1_ref, acc2_ref = rest[2 * pages_per_step:]
    j = pl.program_id(1)
    n_heads, hd2 = kn_ref.shape[-2:]
    q2 = q2_ref[...]
    lane = lax.broadcasted_iota(jnp.int32, (n_heads, LANES), 1)
    head = lax.broadcasted_iota(jnp.int32, (n_heads, LANES), 0)
    own = (lane == head, lane == head + n_heads)
    ones = jnp.ones((LANES, LANES), BF16)

    def spread(x, mp):
        lead = x.shape[:-2]
        sel = jnp.where(own[mp], x, 0.0).astype(BF16).reshape((-1, LANES))
        return _dot(sel, ones).reshape(lead + (n_heads, LANES))

    def spread_f32(x, mp):
        hi = x.astype(BF16).astype(F32)
        return spread(hi, mp) + spread(x - hi, mp)

    def tile2(x):
        return jnp.concatenate([x, x], axis=-1)

    @pl.when(j == 0)
    def _():
        m_ref[...] = _dot(kn_ref[0].astype(BF16), q2)
        l_ref[...] = jnp.ones_like(l_ref)
        acc1_ref[...] = vn_ref[0]
        acc2_ref[...] = vn_ref[0]

    page = k_refs[0].shape[0]
    ss = [_dot(k_ref[...].reshape(page * n_heads, hd2).astype(BF16), q2).reshape(page, n_heads, LANES)
          for k_ref in k_refs]
    m_old = m_ref[...]
    m_new = m_old
    for s in ss:
        m_new = jnp.maximum(m_new, jnp.max(s, axis=0))
    alpha = jnp.exp2(m_old - m_new)
    ps = [jnp.exp2(s - m_new[None]) for s in ss]
    l_new = alpha * l_ref[...]
    for p in ps:
        l_new = l_new + jnp.sum(p, axis=0)
    l_ref[...] = l_new
    m_ref[...] = m_new
    for mp, acc_ref in enumerate((acc1_ref, acc2_ref)):
        acc = tile2(spread_f32(alpha, mp)) * acc_ref[...]
        for p, v_ref in zip(ps, v_refs):
            acc = acc + jnp.sum(tile2(spread(p, mp)) * v_ref[...], axis=0)
        acc_ref[...] = acc

    @pl.when(j == pl.num_programs(1) - 1)
    def _():
        lam = _diff_lambda(lq1_ref, lk1_ref, lq2_ref, lk2_ref, lam_init)
        l = l_ref[...]
        o = acc1_ref[...] / tile2(spread_f32(l, 0)) - lam * (acc2_ref[...] / tile2(spread_f32(l, 1)))
        o_ref[0] = (_rms(o, g_ref[...]) * (1.0 - lam_init)).astype(o_ref.dtype)


def _attn_sample(q, k_new, v_new, cache_k, cache_v, page_table, layer, lam_refs, subln_g, lam_init):
    db, n_heads, hd2 = q.shape
    hd = hd2 // 2
    page = cache_k.shape[2]
    n_pages = page_table.shape[1]
    pps = 8 if n_pages % 8 == 0 else 1
    assert 2 * n_heads <= LANES
    qt = q.reshape(db, n_heads, 2, hd).transpose(0, 2, 3, 1)
    q2 = jnp.zeros((db, 2, hd, 2, n_heads), q.dtype)
    q2 = q2.at[:, 0, :, 0, :].set(qt[:, 0]).at[:, 1, :, 1, :].set(qt[:, 1])
    q2 = jnp.pad(q2.reshape(db, hd2, 2 * n_heads), ((0, 0), (0, 0), (0, LANES - 2 * n_heads))).astype(BF16)
    vec = lambda n: pl.BlockSpec((1, n), lambda b, j, pt: (0, 0))
    tok = pl.BlockSpec((1, n_heads, hd2), lambda b, j, pt: (b, 0, 0))

    def page_spec(p):
        return pl.BlockSpec((None, None, page, n_heads, hd2),
                            lambda b, j, pt: (layer, pt[b, j * pps + p], 0, 0, 0))

    stat = pltpu.VMEM((n_heads, LANES), F32)
    acc = pltpu.VMEM((n_heads, hd2), F32)
    grid_spec = pltpu.PrefetchScalarGridSpec(
        num_scalar_prefetch=1,
        grid=(db, n_pages // pps),
        in_specs=[vec(hd)] * 4 + [vec(hd2), pl.BlockSpec((None, hd2, LANES), lambda b, j, pt: (b, 0, 0)), tok, tok]
                 + [page_spec(p) for p in range(pps)] * 2,
        out_specs=tok,
        scratch_shapes=[stat, stat, acc, acc],
    )
    return pl.pallas_call(
        functools.partial(_attn_sample_kernel, pages_per_step=pps, lam_init=lam_init),
        out_shape=jax.ShapeDtypeStruct((db, n_heads, hd2), BF16),
        grid_spec=grid_spec,
        compiler_params=_params("parallel", "arbitrary"),
        name="diff_attn_sample",
    )(page_table, *lam_refs, subln_g, q2, k_new, v_new, *([cache_k] * pps), *([cache_v] * pps))


def _gated(ya, ob, hn, wcb, wab, wgab, wgbb):
    y_a = _dot(ya, wcb[...])
    y_b = _dot(ob, wab[...])
    g_a = jax.nn.sigmoid(_dot(hn, wgab[...]))
    g_b = jax.nn.sigmoid(_dot(hn, wgbb[...]))
    return g_a * y_a + g_b * y_b


def _merge_kernel(ya_ref, ob_ref, hn_ref, yas_ref, obs_ref, hs_ref, wc_ref, wa_ref, wga_ref, wgb_ref,
                  m_ref, ms_ref, wcb, wab, wgab, wgbb):
    i = pl.program_id(1)
    _cast_weights(i == 0, ((wc_ref, wcb), (wa_ref, wab), (wga_ref, wgab), (wgb_ref, wgbb)))

    @pl.when(i == 0)
    def _():
        ms_ref[...] = _gated(yas_ref[...], obs_ref[...], hs_ref[...], wcb, wab, wgab, wgbb).astype(ms_ref.dtype)

    m_ref[...] = _gated(ya_ref[...], ob_ref[...], hn_ref[...], wcb, wab, wgab, wgbb).astype(m_ref.dtype)


def _merge(ya, ob, hn, yas, obs, hs, w_out_conv, w_out_attn, w_in, gate_col0):
    m, d = hn.shape
    ms = hs.shape[0]
    dc, dm = w_out_conv.shape
    da = w_out_attn.shape[0]
    tm = _pick(m, 512)
    tn = _pick(dm, 512)
    nseg = dm // tn
    g0 = gate_col0 // tn
    row = lambda k: pl.BlockSpec((tm, k), lambda j, i: (i, 0))
    srow = lambda k: pl.BlockSpec((ms, k), lambda j, i: (0, 0))
    wspec = lambda k, off: _resident((k, tn), lambda j, i: (0, off + j))
    return pl.pallas_call(
        _merge_kernel,
        out_shape=(jax.ShapeDtypeStruct((m, dm), BF16), jax.ShapeDtypeStruct((ms, dm), BF16)),
        grid=(nseg, m // tm),
        in_specs=[row(dc), row(da), row(d), srow(dc), srow(da), srow(d),
                  wspec(dc, 0), wspec(da, 0), wspec(d, g0), wspec(d, g0 + nseg)],
        out_specs=(pl.BlockSpec((tm, tn), lambda j, i: (i, j)),
                   pl.BlockSpec((ms, tn), lambda j, i: (0, j))),
        scratch_shapes=[pltpu.VMEM((dc, tn), BF16), pltpu.VMEM((da, tn), BF16),
                        pltpu.VMEM((d, tn), BF16), pltpu.VMEM((d, tn), BF16)],
        compiler_params=_params("arbitrary", "arbitrary"),
        name="gated_merge",
    )(ya, ob, hn, yas, obs, hs, w_out_conv, w_out_attn, w_in, w_in)


def _oproj_kernel(m_ref, x_ref, ms_ref, xs_ref, wo_ref, g_ref, h_ref, f_ref, hs_ref, fs_ref, wob):
    i = pl.program_id(0)
    _cast_weights(i == 0, ((wo_ref, wob),))

    @pl.when(i == 0)
    def _():
        hs = xs_ref[...] + _dot(ms_ref[...], wob[...])
        hs_ref[...] = hs
        fs_ref[...] = _rms(hs, g_ref[...]).astype(fs_ref.dtype)

    h = x_ref[...] + _dot(m_ref[...], wob[...])
    h_ref[...] = h
    f_ref[...] = _rms(h, g_ref[...]).astype(f_ref.dtype)


def _oproj_norm(mg, x, mgs, xs, w_o, g):
    m, d = x.shape
    ms = xs.shape[0]
    dm = mg.shape[1]
    tm = _pick(m, 512)
    whole = lambda r, c: pl.BlockSpec((r, c), lambda i: (0, 0))
    rows = lambda c: pl.BlockSpec((tm, c), lambda i: (i, 0))
    return pl.pallas_call(
        _oproj_kernel,
        out_shape=(jax.ShapeDtypeStruct((m, d), F32), jax.ShapeDtypeStruct((m, d), BF16),
                   jax.ShapeDtypeStruct((ms, d), F32), jax.ShapeDtypeStruct((ms, d), BF16)),
        grid=(m // tm,),
        in_specs=[rows(dm), rows(d), whole(ms, dm), whole(ms, d),
                  _resident(w_o.shape, lambda i: (0, 0)), whole(1, d)],
        out_specs=(rows(d), rows(d), whole(ms, d), whole(ms, d)),
        scratch_shapes=[pltpu.VMEM(w_o.shape, BF16)],
        compiler_params=_params("arbitrary"),
        name="oproj_norm",
    )(mg, x, mgs, xs, w_o, g.reshape(1, d))


def _ffn_kernel(f_ref, h_ref, wg_ref, wu_ref, wd_ref, g_ref, y_ref, acc_ref, *, final_norm):
    c = pl.program_id(1)

    @pl.when(c == 0)
    def _():
        acc_ref[...] = h_ref[...]

    f = f_ref[...]
    a = jax.nn.silu(_dot(f, wg_ref[...])) * _dot(f, wu_ref[...])
    acc_ref[...] += _dot(a.astype(wd_ref.dtype), wd_ref[...])

    @pl.when(c == pl.num_programs(1) - 1)
    def _():
        y_ref[...] = _rms(acc_ref[...], g_ref[...]) if final_norm else acc_ref[...]


def _ffn(f, h, w_gate, w_up, w_down, g_final, final_norm):
    m, d = h.shape
    dff = w_gate.shape[1]
    tm = _pick(m, 512)
    tc = _pick(dff, 512)
    rows = pl.BlockSpec((tm, d), lambda i, c: (i, 0))
    return pl.pallas_call(
        functools.partial(_ffn_kernel, final_norm=final_norm),
        out_shape=jax.ShapeDtypeStruct((m, d), F32),
        grid=(m // tm, dff // tc),
        in_specs=[rows, rows,
                  pl.BlockSpec((d, tc), lambda i, c: (0, c)),
                  pl.BlockSpec((d, tc), lambda i, c: (0, c)),
                  pl.BlockSpec((tc, d), lambda i, c: (c, 0)),
                  pl.BlockSpec((1, d), lambda i, c: (0, 0))],
        out_specs=rows,
        scratch_shapes=[pltpu.VMEM((tm, d), F32)],
        compiler_params=_params("parallel", "arbitrary"),
        name="ffn",
    )(f, h, w_gate, w_up, w_down, g_final.reshape(1, d))


def kernel(x_prompt, x_sample, cache_k, cache_v, state_conv, page_table, w_in, conv_w, w_out_conv, lambda_q1, lambda_k1, lambda_q2, lambda_k2, subln_g, w_out_attn, w_o, norm_mix_g, norm_ffn_g, w_gate, w_up, w_down, norm_final_g):
    batch, seq, d = x_prompt.shape
    db, dseq, _ = x_sample.shape
    assert dseq == 1, "decode rows carry one new token per sequence"
    depth = w_in.shape[0]
    n_heads, hd2 = cache_k.shape[-2:]
    hd = hd2 // 2
    da = n_heads * hd2
    dc = conv_w.shape[1]
    assert state_conv.shape[2] == CONV_WIDTH - 1 and conv_w.shape[2] == CONV_WIDTH
    scale = hd ** -0.5
    qkv_col0 = 3 * dc
    gate_col0 = 3 * dc + 3 * da

    xp = x_prompt.reshape(batch * seq, d)
    xs = x_sample.reshape(db, d)
    outs = [[] for _ in range(6)]
    for l in range(depth):
        lam_init = 0.8 - 0.6 * math.exp(-0.3 * l)
        w_g_b = w_gate[l].astype(BF16)
        w_u_b = w_up[l].astype(BF16)
        w_d_b = w_down[l].astype(BF16)
        cw_t = conv_w[l].T
        lam_refs = [a[l].reshape(1, hd) for a in (lambda_q1, lambda_k1, lambda_q2, lambda_k2)]
        sub_g = subln_g[l].reshape(1, hd2)
        st = state_conv[l]

        hn, hs = _norm_cast(xp, xs, norm_mix_g[l])
        ya, conv_p, ya_s, u_s = _conv_inproj(hn, hs, st[:, 1, :], st[:, 0, :], w_in[l], cw_t, batch, seq)
        q, k32, v32, kb, vb, q_s, k_s, v_s = _qkv_proj(hn, hs, w_in[l], qkv_col0, da, scale * LOG2_E, scale * LOG2_E)
        ob = _attn_prompt(q, kb, vb, lam_refs, sub_g, batch, seq, n_heads, lam_init)
        to_heads = lambda a: a.reshape(db, n_heads, hd2)
        ob_s = _attn_sample(to_heads(q_s), to_heads(k_s), to_heads(v_s), cache_k, cache_v, page_table, l,
                            lam_refs, sub_g, lam_init)
        mg, mg_s = _merge(ya, ob, hn, ya_s, ob_s.reshape(db, da), hs, w_out_conv[l], w_out_attn[l], w_in[l], gate_col0)
        h, f, h_s, f_s = _oproj_norm(mg, xp, mg_s, xs, w_o[l], norm_ffn_g[l])
        xp = _ffn(f, h, w_g_b, w_u_b, w_d_b, norm_final_g, l == depth - 1)
        xs = _ffn(f_s, h_s, w_g_b, w_u_b, w_d_b, norm_final_g, l == depth - 1)

        outs[0].append(k32.reshape(batch, seq, n_heads, hd2))
        outs[1].append(v32.reshape(batch, seq, n_heads, hd2))
        outs[2].append(conv_p)
        outs[3].append(k_s.reshape(db, 1, n_heads, hd2))
        outs[4].append(v_s.reshape(db, 1, n_heads, hd2))
        outs[5].append(jnp.stack([st[:, 1, :], u_s], axis=1))

    y_prompt = xp.reshape(batch, seq, d)
    y_sample = xs.reshape(db, 1, d)
    return (y_prompt, y_sample) + tuple(jnp.stack(o) for o in outs)
```

```python
import functools
import math

import jax
import jax.numpy as jnp
from jax import lax
from jax.experimental import pallas as pl
from jax.experimental.pallas import tpu as pltpu
from jax.experimental.pallas import tpu_sc as plsc

F32 = jnp.float32
BF16 = jnp.bfloat16
EPS = 1e-6
MASK_VALUE = -1e30
CONV_WIDTH = 3
LOG2_E = math.log2(math.e)
V7X_VMEM_LIMIT_BYTES = 56 * 1024 * 1024
LANES = 128
SC_LANES = 16
SC_CORES = 2
SC_SUBCORES = 16
SC_WORKERS = SC_CORES * SC_SUBCORES
SC_CHUNK = 8
SC_PAGE_SHARE = 0.5


def _pick(n, pref):
    t = pref
    while t >= LANES:
        if n % t == 0:
            return t
        t //= 2
    return n


def _params(*sem):
    return pltpu.CompilerParams(dimension_semantics=sem, vmem_limit_bytes=V7X_VMEM_LIMIT_BYTES)


def _rms(x, g):
    return x * lax.rsqrt(jnp.mean(x * x, axis=-1, keepdims=True) + EPS) * g


def _dot(a, b):
    return jnp.dot(a, b, preferred_element_type=F32)


def _dot_nt(a, b):
    return lax.dot_general(a, b, (((1,), (1,)), ((), ())), preferred_element_type=F32)


def _resident(shape, index_map):
    return pl.BlockSpec(shape, index_map, pipeline_mode=pl.Buffered(1))


def _cast_weights(first_step, pairs):
    @pl.when(first_step)
    def _():
        for src, dst in pairs:
            dst[...] = src[...].astype(dst.dtype)


def _norm_kernel(x_ref, xs_ref, g_ref, o_ref, os_ref):
    o_ref[...] = _rms(x_ref[...], g_ref[...]).astype(o_ref.dtype)

    @pl.when(pl.program_id(0) == 0)
    def _():
        os_ref[...] = _rms(xs_ref[...], g_ref[...]).astype(os_ref.dtype)


def _norm_cast(x, xs, g):
    m, d = x.shape
    ms = xs.shape[0]
    tm = _pick(m, 512)
    return pl.pallas_call(
        _norm_kernel,
        out_shape=(jax.ShapeDtypeStruct((m, d), BF16), jax.ShapeDtypeStruct((ms, d), BF16)),
        grid=(m // tm,),
        in_specs=[pl.BlockSpec((tm, d), lambda i: (i, 0)),
                  pl.BlockSpec((ms, d), lambda i: (0, 0)),
                  pl.BlockSpec((1, d), lambda i: (0, 0))],
        out_specs=(pl.BlockSpec((tm, d), lambda i: (i, 0)),
                   pl.BlockSpec((ms, d), lambda i: (0, 0))),
        compiler_params=_params("arbitrary"),
        name="norm_cast",
    )(x, xs, g.reshape(1, d))


def _conv_taps(u, um1, um2, cw_ref):
    return um2 * cw_ref[0:1, :] + um1 * cw_ref[1:2, :] + u * cw_ref[2:3, :]


def _conv_kernel(hn_ref, hs_ref, um1_ref, um2_ref, wb_ref, wc_ref, wh_ref, cw_ref,
                 ya_ref, tail_ref, yas_ref, us_ref, wbb, wcb, whb, carry_ref, *, tiles_per_seq):
    i = pl.program_id(1)
    _cast_weights(i == 0, ((wb_ref, wbb), (wc_ref, wcb), (wh_ref, whb)))

    @pl.when(i == 0)
    def _():
        hs = hs_ref[...]
        us = _dot(hs, wcb[...]) * _dot(hs, whb[...])
        yas_ref[...] = (_dot(hs, wbb[...]) * _conv_taps(us, um1_ref[...], um2_ref[...], cw_ref)).astype(yas_ref.dtype)
        us_ref[...] = us

    @pl.when(i % tiles_per_seq == 0)
    def _():
        carry_ref[...] = jnp.zeros_like(carry_ref)

    hn = hn_ref[...]
    u = _dot(hn, wcb[...]) * _dot(hn, whb[...])
    tm = u.shape[0]
    row = lax.broadcasted_iota(jnp.int32, u.shape, 0)
    c0 = carry_ref[0:1, :]
    c1 = carry_ref[1:2, :]
    um1 = jnp.where(row == 0, c1, pltpu.roll(u, 1, 0))
    um2 = jnp.where(row == 0, c0, jnp.where(row == 1, c1, pltpu.roll(u, 2, 0)))
    yc = _conv_taps(u, um1, um2, cw_ref)
    ya_ref[...] = (_dot(hn, wbb[...]) * yc).astype(ya_ref.dtype)
    last = u[tm - (CONV_WIDTH - 1):, :]
    carry_ref[...] = last
    tail_ref[0] = last


def _conv_inproj(hn, hs, um1, um2, w_in, cw_t, batch, seq):
    m, d = hn.shape
    ms = hs.shape[0]
    dc = cw_t.shape[1]
    tm = _pick(seq, 1024)
    tn = _pick(dc, 512)
    nseg = dc // tn
    wspec = lambda off: _resident((d, tn), lambda j, i: (0, off + j))
    srow = pl.BlockSpec((ms, tn), lambda j, i: (0, j))
    return pl.pallas_call(
        functools.partial(_conv_kernel, tiles_per_seq=seq // tm),
        out_shape=(jax.ShapeDtypeStruct((m, dc), BF16),
                   jax.ShapeDtypeStruct((batch, CONV_WIDTH - 1, dc), F32),
                   jax.ShapeDtypeStruct((ms, dc), BF16),
                   jax.ShapeDtypeStruct((ms, dc), F32)),
        grid=(nseg, m // tm),
        in_specs=[pl.BlockSpec((tm, d), lambda j, i: (i, 0)),
                  pl.BlockSpec((ms, d), lambda j, i: (0, 0)),
                  srow, srow,
                  wspec(0), wspec(nseg), wspec(2 * nseg),
                  pl.BlockSpec((CONV_WIDTH, tn), lambda j, i: (0, j))],
        out_specs=(pl.BlockSpec((tm, tn), lambda j, i: (i, j)),
                   pl.BlockSpec((1, CONV_WIDTH - 1, tn), lambda j, i: (i // (seq // tm), 0, j)),
                   srow, srow),
        scratch_shapes=[pltpu.VMEM((d, tn), BF16)] * 3 + [pltpu.VMEM((CONV_WIDTH - 1, tn), F32)],
        compiler_params=_params("arbitrary", "arbitrary"),
        name="conv_inproj",
    )(hn, hs, um1, um2, w_in, w_in, w_in, cw_t)


def _qkv_kernel(hn_ref, hs_ref, wq_ref, wk_ref, wv_ref, q_ref, k_ref, v_ref, kb_ref, vb_ref,
                qs_ref, ks_ref, vs_ref, wqb, wkb, wvb, *, q_scale, qs_scale):
    i = pl.program_id(1)
    _cast_weights(i == 0, ((wq_ref, wqb), (wk_ref, wkb), (wv_ref, wvb)))

    @pl.when(i == 0)
    def _():
        hs = hs_ref[...]
        qs_ref[...] = _dot(hs, wqb[...]) * qs_scale
        ks_ref[...] = _dot(hs, wkb[...])
        vs_ref[...] = _dot(hs, wvb[...])

    hn = hn_ref[...]
    q_ref[...] = (_dot(hn, wqb[...]) * q_scale).astype(q_ref.dtype)
    k = _dot(hn, wkb[...])
    k_ref[...] = k
    kb_ref[...] = k.astype(kb_ref.dtype)
    v = _dot(hn, wvb[...])
    v_ref[...] = v
    vb_ref[...] = v.astype(vb_ref.dtype)


def _qkv_proj(hn, hs, w_in, col0, da, q_scale, qs_scale):
    m, d = hn.shape
    ms = hs.shape[0]
    tm = _pick(m, 1024)
    tn = _pick(da, 512)
    nseg = da // tn
    c0 = col0 // tn
    blk = pl.BlockSpec((tm, tn), lambda j, i: (i, j))
    srow = pl.BlockSpec((ms, tn), lambda j, i: (0, j))
    wspec = lambda off: _resident((d, tn), lambda j, i: (0, c0 + off + j))
    return pl.pallas_call(
        functools.partial(_qkv_kernel, q_scale=q_scale, qs_scale=qs_scale),
        out_shape=(jax.ShapeDtypeStruct((m, da), BF16),
                   jax.ShapeDtypeStruct((m, da), F32),
                   jax.ShapeDtypeStruct((m, da), F32),
                   jax.ShapeDtypeStruct((m, da), BF16),
                   jax.ShapeDtypeStruct((m, da), BF16),
                   jax.ShapeDtypeStruct((ms, da), F32),
                   jax.ShapeDtypeStruct((ms, da), F32),
                   jax.ShapeDtypeStruct((ms, da), F32)),
        grid=(nseg, m // tm),
        in_specs=[pl.BlockSpec((tm, d), lambda j, i: (i, 0)),
                  pl.BlockSpec((ms, d), lambda j, i: (0, 0)),
                  wspec(0), wspec(nseg), wspec(2 * nseg)],
        out_specs=(blk, blk, blk, blk, blk, srow, srow, srow),
        scratch_shapes=[pltpu.VMEM((d, tn), BF16)] * 3,
        compiler_params=_params("arbitrary", "arbitrary"),
        name="qkv_proj",
    )(hn, hs, w_in, w_in, w_in)


def _diff_lambda(lq1_ref, lk1_ref, lq2_ref, lk2_ref, lam_init):
    a = jnp.sum(lq1_ref[...] * lk1_ref[...], axis=-1, keepdims=True)
    b = jnp.sum(lq2_ref[...] * lk2_ref[...], axis=-1, keepdims=True)
    return jnp.exp(a) - jnp.exp(b) + lam_init


def _attn_prompt_kernel(lq1_ref, lk1_ref, lq2_ref, lk2_ref, g_ref, q_ref, k_ref, v_ref, o_ref,
                        s_ref, p_ref, c_ref, *, tq, rc, lam_init):
    seq = q_ref.shape[0]
    hd = q_ref.shape[1] // 2
    nq = seq // tq
    lam = _diff_lambda(lq1_ref, lk1_ref, lq2_ref, lk2_ref, lam_init)

    def scores(qi):
        slot = qi % 2
        lo, hi = qi * tq, (qi + 1) * tq
        q = q_ref[lo:hi, :]
        for mp in range(2):
            cols = slice(mp * hd, (mp + 1) * hd)
            if qi > 0:
                s_ref[slot, mp, :, :lo] = _dot_nt(q[:, cols], k_ref[:lo, cols])
            sd = _dot_nt(q[:, cols], k_ref[lo:hi, cols])
            keep = (lax.broadcasted_iota(jnp.int32, sd.shape, 1) <= lax.broadcasted_iota(jnp.int32, sd.shape, 0))
            s_ref[slot, mp, :, lo:hi] = jnp.where(keep, sd, MASK_VALUE)

    def softmax_diff(qi):
        slot = qi % 2
        hi = (qi + 1) * tq
        for r in range(tq // rc):
            rows = slice(r * rc, (r + 1) * rc)
            x1 = s_ref[slot, 0, rows, :hi]
            x2 = s_ref[slot, 1, rows, :hi]
            e1 = jnp.exp2(x1 - jnp.max(x1, axis=-1, keepdims=True))
            e2 = jnp.exp2(x2 - jnp.max(x2, axis=-1, keepdims=True))
            l1 = jnp.sum(e1, axis=-1, keepdims=True)
            l2 = jnp.sum(e2, axis=-1, keepdims=True)
            p_ref[slot, rows, :hi] = (e1 - e2 * (lam * l1 / l2)).astype(p_ref.dtype)
            c_ref[slot, rows, :] = 1.0 / l1

    def values(qi):
        slot = qi % 2
        lo, hi = qi * tq, (qi + 1) * tq
        o = _dot(p_ref[slot, :, :hi], v_ref[:hi, :]) * c_ref[slot]
        o_ref[lo:hi, :] = (_rms(o, g_ref[...]) * (1.0 - lam_init)).astype(o_ref.dtype)

    scores(0)
    for qi in range(nq):
        if qi + 1 < nq:
            scores(qi + 1)
        softmax_diff(qi)
        values(qi)


def _attn_prompt(q, kb, vb, lam_refs, subln_g, batch, seq, n_heads, lam_init):
    m, da = q.shape
    hd2 = da // n_heads
    tq = _pick(seq, 256)
    rc = 32
    vec = lambda n: pl.BlockSpec((1, n), lambda b, h: (0, 0))
    head = pl.BlockSpec((seq, hd2), lambda b, h: (b, h))
    return pl.pallas_call(
        functools.partial(_attn_prompt_kernel, tq=tq, rc=rc, lam_init=lam_init),
        out_shape=jax.ShapeDtypeStruct((m, da), BF16),
        grid=(batch, n_heads),
        in_specs=[vec(hd2 // 2)] * 4 + [vec(hd2), head, head, head],
        out_specs=head,
        scratch_shapes=[pltpu.VMEM((2, 2, tq, seq), F32), pltpu.VMEM((2, tq, seq), BF16),
                        pltpu.VMEM((2, tq, 1), F32)],
        compiler_params=_params("parallel", "parallel"),
        name="diff_attn_prompt",
    )(*lam_refs, subln_g, q, kb, vb)


def _attn_sample_tc_kernel(pt_ref, q2_ref, kn_ref, vn_ref, *rest, pages_per_step):
    del pt_ref
    k_refs = rest[:pages_per_step]
    v_refs = rest[pages_per_step:2 * pages_per_step]
    m_ref, l_ref, acc1_ref, acc2_ref = (r.at[0] for r in rest[2 * pages_per_step:])
    j = pl.program_id(1)
    n_heads, hd2 = kn_ref.shape[-2:]
    q2 = q2_ref[...]
    lane = lax.broadcasted_iota(jnp.int32, (n_heads, LANES), 1)
    head = lax.broadcasted_iota(jnp.int32, (n_heads, LANES), 0)
    own = (lane == head, lane == head + n_heads)
    ones = jnp.ones((LANES, LANES), BF16)

    def spread(x, mp):
        lead = x.shape[:-2]
        sel = jnp.where(own[mp], x, 0.0).astype(BF16).reshape((-1, LANES))
        return _dot(sel, ones).reshape(lead + (n_heads, LANES))

    def spread_f32(x, mp):
        hi = x.astype(BF16).astype(F32)
        return spread(hi, mp) + spread(x - hi, mp)

    def tile2(x):
        return jnp.concatenate([x, x], axis=-1)

    @pl.when(j == 0)
    def _():
        m_ref[...] = _dot(kn_ref[0].astype(BF16), q2)
        l_ref[...] = jnp.ones_like(l_ref)
        acc1_ref[...] = vn_ref[0]
        acc2_ref[...] = vn_ref[0]

    page = k_refs[0].shape[0]
    ss = [_dot(k_ref[...].reshape(page * n_heads, hd2).astype(BF16), q2).reshape(page, n_heads, LANES)
          for k_ref in k_refs]
    m_old = m_ref[...]
    m_new = m_old
    for s in ss:
        m_new = jnp.maximum(m_new, jnp.max(s, axis=0))
    alpha = jnp.exp2(m_old - m_new)
    ps = [jnp.exp2(s - m_new[None]) for s in ss]
    l_new = alpha * l_ref[...]
    for p in ps:
        l_new = l_new + jnp.sum(p, axis=0)
    l_ref[...] = l_new
    m_ref[...] = m_new
    for mp, acc_ref in enumerate((acc1_ref, acc2_ref)):
        acc = tile2(spread_f32(alpha, mp)) * acc_ref[...]
        for p, v_ref in zip(ps, v_refs):
            acc = acc + jnp.sum(tile2(spread(p, mp)) * v_ref[...], axis=0)
        acc_ref[...] = acc


def _attn_sample_tc(q, k_new, v_new, cache_k, cache_v, page_table, layer, first_page):
    db, n_heads, hd2 = q.shape
    hd = hd2 // 2
    page = cache_k.shape[2]
    n_pages = page_table.shape[1] - first_page
    pps = 8 if n_pages % 8 == 0 else 1
    assert 2 * n_heads <= LANES and n_pages > 0
    qt = (q * LOG2_E).reshape(db, n_heads, 2, hd).transpose(0, 2, 3, 1)
    q2 = jnp.zeros((db, 2, hd, 2, n_heads), q.dtype)
    q2 = q2.at[:, 0, :, 0, :].set(qt[:, 0]).at[:, 1, :, 1, :].set(qt[:, 1])
    q2 = jnp.pad(q2.reshape(db, hd2, 2 * n_heads), ((0, 0), (0, 0), (0, LANES - 2 * n_heads))).astype(BF16)
    tok = pl.BlockSpec((1, n_heads, hd2), lambda b, j, pt: (b, 0, 0))
    stat = pl.BlockSpec((1, n_heads, LANES), lambda b, j, pt: (b, 0, 0))

    def page_spec(p):
        return pl.BlockSpec((None, None, page, n_heads, hd2),
                            lambda b, j, pt: (layer, pt[b, first_page + j * pps + p], 0, 0, 0))

    grid_spec = pltpu.PrefetchScalarGridSpec(
        num_scalar_prefetch=1,
        grid=(db, n_pages // pps),
        in_specs=[pl.BlockSpec((None, hd2, LANES), lambda b, j, pt: (b, 0, 0)), tok, tok]
                 + [page_spec(p) for p in range(pps)] * 2,
        out_specs=(stat, stat, tok, tok),
    )
    return pl.pallas_call(
        functools.partial(_attn_sample_tc_kernel, pages_per_step=pps),
        out_shape=(jax.ShapeDtypeStruct((db, n_heads, LANES), F32), jax.ShapeDtypeStruct((db, n_heads, LANES), F32),
                   jax.ShapeDtypeStruct((db, n_heads, hd2), F32), jax.ShapeDtypeStruct((db, n_heads, hd2), F32)),
        grid_spec=grid_spec,
        compiler_params=_params("parallel", "arbitrary"),
        name="diff_attn_sample_tc",
    )(page_table, q2, k_new, v_new, *([cache_k] * pps), *([cache_v] * pps))


def _attn_sample_sc(q, cache_k, cache_v, page_table, layer, n_pages):
    db, n_heads, hd2 = q.shape
    hd = hd2 // 2
    depth, nphys, page = cache_k.shape[:3]
    cpp = page // SC_CHUNK
    n_chunks = n_pages * cpp
    assert db == SC_WORKERS and page % SC_CHUNK == 0 and n_chunks % 2 == 0 and page_table.shape[1] % SC_LANES == 0
    nv = hd // SC_LANES
    ck = cache_k.reshape(depth * nphys * page, n_heads, hd2)
    cv = cache_v.reshape(depth * nphys * page, n_heads, hd2)
    mesh = plsc.VectorSubcoreMesh(core_axis_name="c", subcore_axis_name="s",
                                  num_cores=SC_CORES, num_subcores=SC_SUBCORES)
    buf = pltpu.VMEM((SC_CHUNK, n_heads, hd2), F32)
    idx_t = pltpu.VMEM((SC_LANES,), jnp.int32)
    sem = pltpu.SemaphoreType.DMA

    @functools.partial(
        pl.kernel, mesh=mesh, name="diff_attn_sample_sc",
        compiler_params=pltpu.CompilerParams(use_tc_tiling_on_sc=True, needs_layout_passes=False),
        out_type=(jax.ShapeDtypeStruct((db, 2, n_heads, hd2), F32),
                  jax.ShapeDtypeStruct((db, 2 * n_heads, SC_LANES), F32),
                  jax.ShapeDtypeStruct((db, 2 * n_heads, SC_LANES), F32)),
        scratch_types=[buf, buf, buf, buf, pltpu.VMEM((n_heads, hd2), F32), pltpu.VMEM((2, n_heads, hd2), F32),
                       pltpu.VMEM((2 * n_heads, SC_LANES), F32), pltpu.VMEM((2 * n_heads, SC_LANES), F32),
                       pltpu.VMEM((page_table.shape[1],), jnp.int32), idx_t, idx_t, sem, sem, sem, sem],
    )
    def sc_kernel(q_hbm, ck_hbm, cv_hbm, pt_hbm, acc_hbm, m_hbm, l_hbm,
                  k0, k1, v0, v1, q_v, acc_v, m_v, l_v, pt_v, idx0, idx1, sk0, sk1, sv0, sv1):
        b = lax.axis_index("s") * SC_CORES + lax.axis_index("c")
        pltpu.sync_copy(pt_hbm.at[b], pt_v)
        pltpu.sync_copy(q_hbm.at[b], q_v)
        lanes = lax.iota(jnp.int32, SC_LANES)
        zero = jnp.zeros((SC_LANES,), F32)
        for r in range(2 * n_heads):
            m_v[r, :] = jnp.full((SC_LANES,), -jnp.inf, F32)
            l_v[r, :] = zero
        for mp in range(2):
            for h in range(n_heads):
                for i in range(hd2 // SC_LANES):
                    acc_v[mp, h, pl.ds(i * SC_LANES, SC_LANES)] = zero
        slots = ((k0, v0, idx0, sk0, sv0), (k1, v1, idx1, sk1, sv1))

        def copies(slot):
            kb, vb, idx, sk, sv = slots[slot]
            rows = idx.at[pl.ds(0, SC_CHUNK)]
            return (pltpu.make_async_copy(ck_hbm.at[rows], kb, sk), pltpu.make_async_copy(cv_hbm.at[rows], vb, sv))

        def issue(c, slot):
            j = c // cpp
            ptv = pt_v[pl.ds(pl.multiple_of((j // SC_LANES) * SC_LANES, SC_LANES), SC_LANES)]
            pg = ptv.at[jnp.full((SC_LANES,), j % SC_LANES, jnp.int32)].get(mode="promise_in_bounds")
            slots[slot][2][...] = (layer * nphys + pg) * page + (c % cpp) * SC_CHUNK + lanes
            for cpy in copies(slot):
                cpy.start()

        def wait(slot):
            for cpy in copies(slot):
                cpy.wait()

        def fold(slot):
            kb, vb = slots[slot][:2]

            @pl.loop(0, n_heads)
            def _(h):
                ps = []
                alphas = []
                for mp in range(2):
                    qs = [q_v[h, pl.ds(mp * hd + i * SC_LANES, SC_LANES)] for i in range(nv)]
                    svec = jnp.full((SC_LANES,), -jnp.inf, F32)
                    for t in range(SC_CHUNK):
                        a = qs[0] * kb[t, h, pl.ds(mp * hd, SC_LANES)]
                        for i in range(1, nv):
                            a = a + qs[i] * kb[t, h, pl.ds(mp * hd + i * SC_LANES, SC_LANES)]
                        svec = jnp.where(lanes == t, jnp.sum(a), svec)
                    row = mp * n_heads + h
                    m_old = m_v[row, :]
                    m_new = jnp.maximum(m_old, jnp.max(svec))
                    alpha = jnp.exp(m_old - m_new)
                    p = jnp.exp(svec - m_new)
                    l_v[row, :] = alpha * l_v[row, :] + jnp.sum(p)
                    m_v[row, :] = m_new
                    ps.append(p)
                    alphas.append(alpha)
                for i in range(hd2 // SC_LANES):
                    cols = pl.ds(i * SC_LANES, SC_LANES)
                    a1 = alphas[0] * acc_v[0, h, cols]
                    a2 = alphas[1] * acc_v[1, h, cols]
                    for t in range(SC_CHUNK):
                        v = vb[t, h, cols]
                        a1 = a1 + ps[0][t] * v
                        a2 = a2 + ps[1][t] * v
                    acc_v[0, h, cols] = a1
                    acc_v[1, h, cols] = a2

        issue(0, 0)

        @pl.loop(0, n_chunks, step=2)
        def _(c):
            issue(c + 1, 1)
            wait(0)
            fold(0)

            @pl.when(c + 2 < n_chunks)
            def _():
                issue(c + 2, 0)

            wait(1)
            fold(1)

        pltpu.sync_copy(acc_v, acc_hbm.at[b])
        pltpu.sync_copy(m_v, m_hbm.at[b])
        pltpu.sync_copy(l_v, l_hbm.at[b])

    return sc_kernel(q, ck, cv, page_table)


def _attn_finish_kernel(lq1_ref, lk1_ref, lq2_ref, lk2_ref, g_ref, mt_ref, lt_ref, a1_ref, a2_ref,
                        as_ref, ms_ref, ls_ref, o_ref, *, lam_init):
    n_heads = mt_ref.shape[1]
    lam = _diff_lambda(lq1_ref, lk1_ref, lq2_ref, lk2_ref, lam_init)
    lane = lax.broadcasted_iota(jnp.int32, mt_ref.shape, 2)
    head = lax.broadcasted_iota(jnp.int32, mt_ref.shape, 1)
    outs = []
    for mp, acc_t_ref in enumerate((a1_ref, a2_ref)):
        own = lane == head + mp * n_heads
        m_t = jnp.sum(jnp.where(own, mt_ref[...], 0.0), axis=-1, keepdims=True)
        l_t = jnp.sum(jnp.where(own, lt_ref[...], 0.0), axis=-1, keepdims=True)
        rows = slice(mp * n_heads, (mp + 1) * n_heads)
        m_s = ms_ref[:, rows, 0:1] * LOG2_E
        l_s = ls_ref[:, rows, 0:1]
        m = jnp.maximum(m_t, m_s)
        w_t = jnp.exp2(m_t - m)
        w_s = jnp.exp2(m_s - m)
        outs.append((w_t * acc_t_ref[...] + w_s * as_ref[:, mp]) / (w_t * l_t + w_s * l_s))
    o = outs[0] - lam * outs[1]
    o_ref[...] = (_rms(o, g_ref[...]) * (1.0 - lam_init)).astype(o_ref.dtype)


def _attn_finish(tc_state, sc_state, lam_refs, subln_g, lam_init):
    m_t, l_t, a1, a2 = tc_state
    acc_s, m_s, l_s = sc_state
    return pl.pallas_call(
        functools.partial(_attn_finish_kernel, lam_init=lam_init),
        out_shape=jax.ShapeDtypeStruct(a1.shape, BF16),
        compiler_params=_params(),
        name="diff_attn_sample_finish",
    )(*lam_refs, subln_g, m_t, l_t, a1, a2, acc_s, m_s, l_s)


def _gated(ya, ob, hn, wcb, wab, wgab, wgbb):
    y_a = _dot(ya, wcb[...])
    y_b = _dot(ob, wab[...])
    g_a = jax.nn.sigmoid(_dot(hn, wgab[...]))
    g_b = jax.nn.sigmoid(_dot(hn, wgbb[...]))
    return g_a * y_a + g_b * y_b


def _merge_kernel(ya_ref, ob_ref, hn_ref, yas_ref, obs_ref, hs_ref, wc_ref, wa_ref, wga_ref, wgb_ref,
                  m_ref, ms_ref, wcb, wab, wgab, wgbb):
    i = pl.program_id(1)
    _cast_weights(i == 0, ((wc_ref, wcb), (wa_ref, wab), (wga_ref, wgab), (wgb_ref, wgbb)))

    @pl.when(i == 0)
    def _():
        ms_ref[...] = _gated(yas_ref[...], obs_ref[...], hs_ref[...], wcb, wab, wgab, wgbb).astype(ms_ref.dtype)

    m_ref[...] = _gated(ya_ref[...], ob_ref[...], hn_ref[...], wcb, wab, wgab, wgbb).astype(m_ref.dtype)


def _merge(ya, ob, hn, yas, obs, hs, w_out_conv, w_out_attn, w_in, gate_col0):
    m, d = hn.shape
    ms = hs.shape[0]
    dc, dm = w_out_conv.shape
    da = w_out_attn.shape[0]
    tm = _pick(m, 512)
    tn = _pick(dm, 512)
    nseg = dm // tn
    g0 = gate_col0 // tn
    row = lambda k: pl.BlockSpec((tm, k), lambda j, i: (i, 0))
    srow = lambda k: pl.BlockSpec((ms, k), lambda j, i: (0, 0))
    wspec = lambda k, off: _resident((k, tn), lambda j, i: (0, off + j))
    return pl.pallas_call(
        _merge_kernel,
        out_shape=(jax.ShapeDtypeStruct((m, dm), BF16), jax.ShapeDtypeStruct((ms, dm), BF16)),
        grid=(nseg, m // tm),
        in_specs=[row(dc), row(da), row(d), srow(dc), srow(da), srow(d),
                  wspec(dc, 0), wspec(da, 0), wspec(d, g0), wspec(d, g0 + nseg)],
        out_specs=(pl.BlockSpec((tm, tn), lambda j, i: (i, j)),
                   pl.BlockSpec((ms, tn), lambda j, i: (0, j))),
        scratch_shapes=[pltpu.VMEM((dc, tn), BF16), pltpu.VMEM((da, tn), BF16),
                        pltpu.VMEM((d, tn), BF16), pltpu.VMEM((d, tn), BF16)],
        compiler_params=_params("arbitrary", "arbitrary"),
        name="gated_merge",
    )(ya, ob, hn, yas, obs, hs, w_out_conv, w_out_attn, w_in, w_in)


def _oproj_kernel(m_ref, x_ref, ms_ref, xs_ref, wo_ref, g_ref, h_ref, f_ref, hs_ref, fs_ref, wob):
    i = pl.program_id(0)
    _cast_weights(i == 0, ((wo_ref, wob),))

    @pl.when(i == 0)
    def _():
        hs = xs_ref[...] + _dot(ms_ref[...], wob[...])
        hs_ref[...] = hs
        fs_ref[...] = _rms(hs, g_ref[...]).astype(fs_ref.dtype)

    h = x_ref[...] + _dot(m_ref[...], wob[...])
    h_ref[...] = h
    f_ref[...] = _rms(h, g_ref[...]).astype(f_ref.dtype)


def _oproj_norm(mg, x, mgs, xs, w_o, g):
    m, d = x.shape
    ms = xs.shape[0]
    dm = mg.shape[1]
    tm = _pick(m, 512)
    whole = lambda r, c: pl.BlockSpec((r, c), lambda i: (0, 0))
    rows = lambda c: pl.BlockSpec((tm, c), lambda i: (i, 0))
    return pl.pallas_call(
        _oproj_kernel,
        out_shape=(jax.ShapeDtypeStruct((m, d), F32), jax.ShapeDtypeStruct((m, d), BF16),
                   jax.ShapeDtypeStruct((ms, d), F32), jax.ShapeDtypeStruct((ms, d), BF16)),
        grid=(m // tm,),
        in_specs=[rows(dm), rows(d), whole(ms, dm), whole(ms, d),
                  _resident(w_o.shape, lambda i: (0, 0)), whole(1, d)],
        out_specs=(rows(d), rows(d), whole(ms, d), whole(ms, d)),
        scratch_shapes=[pltpu.VMEM(w_o.shape, BF16)],
        compiler_params=_params("arbitrary"),
        name="oproj_norm",
    )(mg, x, mgs, xs, w_o, g.reshape(1, d))


def _ffn_kernel(f_ref, h_ref, wg_ref, wu_ref, wd_ref, g_ref, y_ref, acc_ref, *, final_norm):
    c = pl.program_id(1)

    @pl.when(c == 0)
    def _():
        acc_ref[...] = h_ref[...]

    f = f_ref[...]
    a = jax.nn.silu(_dot(f, wg_ref[...])) * _dot(f, wu_ref[...])
    acc_ref[...] += _dot(a.astype(wd_ref.dtype), wd_ref[...])

    @pl.when(c == pl.num_programs(1) - 1)
    def _():
        y_ref[...] = _rms(acc_ref[...], g_ref[...]) if final_norm else acc_ref[...]


def _ffn(f, h, w_gate, w_up, w_down, g_final, final_norm):
    m, d = h.shape
    dff = w_gate.shape[1]
    tm = _pick(m, 512)
    tc = _pick(dff, 512)
    rows = pl.BlockSpec((tm, d), lambda i, c: (i, 0))
    return pl.pallas_call(
        functools.partial(_ffn_kernel, final_norm=final_norm),
        out_shape=jax.ShapeDtypeStruct((m, d), F32),
        grid=(m // tm, dff // tc),
        in_specs=[rows, rows,
                  pl.BlockSpec((d, tc), lambda i, c: (0, c)),
                  pl.BlockSpec((d, tc), lambda i, c: (0, c)),
                  pl.BlockSpec((tc, d), lambda i, c: (c, 0)),
                  pl.BlockSpec((1, d), lambda i, c: (0, 0))],
        out_specs=rows,
        scratch_shapes=[pltpu.VMEM((tm, d), F32)],
        compiler_params=_params("parallel", "arbitrary"),
        name="ffn",
    )(f, h, w_gate, w_up, w_down, g_final.reshape(1, d))


def kernel(x_prompt, x_sample, cache_k, cache_v, state_conv, page_table, w_in, conv_w, w_out_conv, lambda_q1, lambda_k1, lambda_q2, lambda_k2, subln_g, w_out_attn, w_o, norm_mix_g, norm_ffn_g, w_gate, w_up, w_down, norm_final_g):
    batch, seq, d = x_prompt.shape
    db, dseq, _ = x_sample.shape
    assert dseq == 1, "decode rows carry one new token per sequence"
    depth = w_in.shape[0]
    n_heads, hd2 = cache_k.shape[-2:]
    hd = hd2 // 2
    da = n_heads * hd2
    dc = conv_w.shape[1]
    assert state_conv.shape[2] == CONV_WIDTH - 1 and conv_w.shape[2] == CONV_WIDTH
    scale = hd ** -0.5
    qkv_col0 = 3 * dc
    gate_col0 = 3 * dc + 3 * da
    n_pages = page_table.shape[1]
    sc_pages = 8 * int(n_pages * SC_PAGE_SHARE / 8)
    if db != SC_WORKERS or n_pages % SC_LANES or cache_k.shape[2] % SC_CHUNK or sc_pages >= n_pages:
        sc_pages = 0

    xp = x_prompt.reshape(batch * seq, d)
    xs = x_sample.reshape(db, d)
    outs = [[] for _ in range(6)]
    for l in range(depth):
        lam_init = 0.8 - 0.6 * math.exp(-0.3 * l)
        w_g_b = w_gate[l].astype(BF16)
        w_u_b = w_up[l].astype(BF16)
        w_d_b = w_down[l].astype(BF16)
        cw_t = conv_w[l].T
        lam_refs = [a[l].reshape(1, hd) for a in (lambda_q1, lambda_k1, lambda_q2, lambda_k2)]
        sub_g = subln_g[l].reshape(1, hd2)
        st = state_conv[l]

        hn, hs = _norm_cast(xp, xs, norm_mix_g[l])
        q, k32, v32, kb, vb, q_s, k_s, v_s = _qkv_proj(hn, hs, w_in[l], qkv_col0, da, scale * LOG2_E, scale)
        to_heads = lambda a: a.reshape(db, n_heads, hd2)
        q_s3, k_s3, v_s3 = to_heads(q_s), to_heads(k_s), to_heads(v_s)
        if sc_pages:
            sc_state = _attn_sample_sc(q_s3, cache_k, cache_v, page_table, l, sc_pages)
        else:
            sc_state = (jnp.zeros((db, 2, n_heads, hd2), F32), jnp.full((db, 2 * n_heads, SC_LANES), -jnp.inf, F32),
                        jnp.zeros((db, 2 * n_heads, SC_LANES), F32))
        ya, conv_p, ya_s, u_s = _conv_inproj(hn, hs, st[:, 1, :], st[:, 0, :], w_in[l], cw_t, batch, seq)
        ob = _attn_prompt(q, kb, vb, lam_refs, sub_g, batch, seq, n_heads, lam_init)
        tc_state = _attn_sample_tc(q_s3, k_s3, v_s3, cache_k, cache_v, page_table, l, sc_pages)
        ob_s = _attn_finish(tc_state, sc_state, lam_refs, sub_g, lam_init)
        mg, mg_s = _merge(ya, ob, hn, ya_s, ob_s.reshape(db, da), hs, w_out_conv[l], w_out_attn[l], w_in[l], gate_col0)
        h, f, h_s, f_s = _oproj_norm(mg, xp, mg_s, xs, w_o[l], norm_ffn_g[l])
        xp = _ffn(f, h, w_g_b, w_u_b, w_d_b, norm_final_g, l == depth - 1)
        xs = _ffn(f_s, h_s, w_g_b, w_u_b, w_d_b, norm_final_g, l == depth - 1)

        outs[0].append(k32.reshape(batch, seq, n_heads, hd2))
        outs[1].append(v32.reshape(batch, seq, n_heads, hd2))
        outs[2].append(conv_p)
        outs[3].append(k_s.reshape(db, 1, n_heads, hd2))
        outs[4].append(v_s.reshape(db, 1, n_heads, hd2))
        outs[5].append(jnp.stack([st[:, 1, :], u_s], axis=1))

    y_prompt = xp.reshape(batch, seq, d)
    y_sample = xs.reshape(db, 1, d)
    return (y_prompt, y_sample) + tuple(jnp.stack(o) for o in outs)
```

```python
import functools
import math

import jax
import jax.numpy as jnp
from jax import lax
from jax.experimental import pallas as pl
from jax.experimental.pallas import tpu as pltpu
from jax.experimental.pallas import tpu_sc as plsc

F32 = jnp.float32
BF16 = jnp.bfloat16
EPS = 1e-6
MASK_VALUE = -1e30
CONV_WIDTH = 3
LOG2_E = math.log2(math.e)
V7X_VMEM_LIMIT_BYTES = 56 * 1024 * 1024
LANES = 128
SC_LANES = 16
SC_CORES = 2
SC_SUBCORES = 16
SC_WORKERS = SC_CORES * SC_SUBCORES
SC_CHUNK = 8


def _pick(n, pref):
    t = pref
    while t >= LANES:
        if n % t == 0:
            return t
        t //= 2
    return n


def _params(*sem):
    return pltpu.CompilerParams(dimension_semantics=sem, vmem_limit_bytes=V7X_VMEM_LIMIT_BYTES)


def _rms(x, g):
    return x * lax.rsqrt(jnp.mean(x * x, axis=-1, keepdims=True) + EPS) * g


def _dot(a, b):
    return jnp.dot(a, b, preferred_element_type=F32)


def _dot_nt(a, b):
    return lax.dot_general(a, b, (((1,), (1,)), ((), ())), preferred_element_type=F32)


def _resident(shape, index_map):
    return pl.BlockSpec(shape, index_map, pipeline_mode=pl.Buffered(1))


def _cast_weights(first_step, pairs):
    @pl.when(first_step)
    def _():
        for src, dst in pairs:
            dst[...] = src[...].astype(dst.dtype)


def _norm_kernel(x_ref, xs_ref, g_ref, o_ref, os_ref):
    o_ref[...] = _rms(x_ref[...], g_ref[...]).astype(o_ref.dtype)

    @pl.when(pl.program_id(0) == 0)
    def _():
        os_ref[...] = _rms(xs_ref[...], g_ref[...]).astype(os_ref.dtype)


def _norm_cast(x, xs, g):
    m, d = x.shape
    ms = xs.shape[0]
    tm = _pick(m, 512)
    return pl.pallas_call(
        _norm_kernel,
        out_shape=(jax.ShapeDtypeStruct((m, d), BF16), jax.ShapeDtypeStruct((ms, d), BF16)),
        grid=(m // tm,),
        in_specs=[pl.BlockSpec((tm, d), lambda i: (i, 0)),
                  pl.BlockSpec((ms, d), lambda i: (0, 0)),
                  pl.BlockSpec((1, d), lambda i: (0, 0))],
        out_specs=(pl.BlockSpec((tm, d), lambda i: (i, 0)),
                   pl.BlockSpec((ms, d), lambda i: (0, 0))),
        compiler_params=_params("arbitrary"),
        name="norm_cast",
    )(x, xs, g.reshape(1, d))


def _conv_taps(u, um1, um2, cw_ref):
    return um2 * cw_ref[0:1, :] + um1 * cw_ref[1:2, :] + u * cw_ref[2:3, :]


def _conv_kernel(hn_ref, hs_ref, um1_ref, um2_ref, wb_ref, wc_ref, wh_ref, wga_ref, wgb_ref, cw_ref,
                 ya_ref, ga_ref, gb_ref, tail_ref, yas_ref, gas_ref, gbs_ref, us_ref,
                 wbb, wcb, whb, wgab, wgbb, carry_ref, *, tiles_per_seq):
    i = pl.program_id(1)
    _cast_weights(i == 0, ((wb_ref, wbb), (wc_ref, wcb), (wh_ref, whb), (wga_ref, wgab), (wgb_ref, wgbb)))

    @pl.when(i == 0)
    def _():
        hs = hs_ref[...]
        us = _dot(hs, wcb[...]) * _dot(hs, whb[...])
        yas_ref[...] = (_dot(hs, wbb[...]) * _conv_taps(us, um1_ref[...], um2_ref[...], cw_ref)).astype(yas_ref.dtype)
        us_ref[...] = us
        gas_ref[...] = jax.nn.sigmoid(_dot(hs, wgab[...])).astype(gas_ref.dtype)
        gbs_ref[...] = jax.nn.sigmoid(_dot(hs, wgbb[...])).astype(gbs_ref.dtype)

    @pl.when(i % tiles_per_seq == 0)
    def _():
        carry_ref[...] = jnp.zeros_like(carry_ref)

    hn = hn_ref[...]
    u = _dot(hn, wcb[...]) * _dot(hn, whb[...])
    tm = u.shape[0]
    row = lax.broadcasted_iota(jnp.int32, u.shape, 0)
    c0 = carry_ref[0:1, :]
    c1 = carry_ref[1:2, :]
    um1 = jnp.where(row == 0, c1, pltpu.roll(u, 1, 0))
    um2 = jnp.where(row == 0, c0, jnp.where(row == 1, c1, pltpu.roll(u, 2, 0)))
    yc = _conv_taps(u, um1, um2, cw_ref)
    ya_ref[...] = (_dot(hn, wbb[...]) * yc).astype(ya_ref.dtype)
    last = u[tm - (CONV_WIDTH - 1):, :]
    carry_ref[...] = last
    tail_ref[0] = last
    ga_ref[...] = jax.nn.sigmoid(_dot(hn, wgab[...])).astype(ga_ref.dtype)
    gb_ref[...] = jax.nn.sigmoid(_dot(hn, wgbb[...])).astype(gb_ref.dtype)


def _conv_inproj(hn, hs, um1, um2, w_in, cw_t, gate_col0, batch, seq):
    m, d = hn.shape
    ms = hs.shape[0]
    dc = cw_t.shape[1]
    tm = _pick(seq, 1024)
    tn = _pick(dc, 512)
    nseg = dc // tn
    g0 = gate_col0 // tn
    wspec = lambda off: _resident((d, tn), lambda j, i: (0, off + j))
    blk = pl.BlockSpec((tm, tn), lambda j, i: (i, j))
    srow = pl.BlockSpec((ms, tn), lambda j, i: (0, j))
    act = lambda rows: jax.ShapeDtypeStruct((rows, dc), BF16)
    return pl.pallas_call(
        functools.partial(_conv_kernel, tiles_per_seq=seq // tm),
        out_shape=(act(m), act(m), act(m),
                   jax.ShapeDtypeStruct((batch, CONV_WIDTH - 1, dc), F32),
                   act(ms), act(ms), act(ms),
                   jax.ShapeDtypeStruct((ms, dc), F32)),
        grid=(nseg, m // tm),
        in_specs=[pl.BlockSpec((tm, d), lambda j, i: (i, 0)),
                  pl.BlockSpec((ms, d), lambda j, i: (0, 0)),
                  srow, srow,
                  wspec(0), wspec(nseg), wspec(2 * nseg), wspec(g0), wspec(g0 + nseg),
                  pl.BlockSpec((CONV_WIDTH, tn), lambda j, i: (0, j))],
        out_specs=(blk, blk, blk,
                   pl.BlockSpec((1, CONV_WIDTH - 1, tn), lambda j, i: (i // (seq // tm), 0, j)),
                   srow, srow, srow, srow),
        scratch_shapes=[pltpu.VMEM((d, tn), BF16)] * 5 + [pltpu.VMEM((CONV_WIDTH - 1, tn), F32)],
        compiler_params=_params("arbitrary", "arbitrary"),
        name="conv_inproj",
    )(hn, hs, um1, um2, w_in, w_in, w_in, w_in, w_in, cw_t)


def _qkv_kernel(hn_ref, hs_ref, wq_ref, wk_ref, wv_ref, q_ref, k_ref, v_ref, kb_ref, vb_ref,
                qs_ref, ks_ref, vs_ref, wqb, wkb, wvb, *, q_scale, qs_scale):
    i = pl.program_id(1)
    _cast_weights(i == 0, ((wq_ref, wqb), (wk_ref, wkb), (wv_ref, wvb)))

    @pl.when(i == 0)
    def _():
        hs = hs_ref[...]
        qs_ref[...] = _dot(hs, wqb[...]) * qs_scale
        ks_ref[...] = _dot(hs, wkb[...])
        vs_ref[...] = _dot(hs, wvb[...])

    hn = hn_ref[...]
    q_ref[...] = (_dot(hn, wqb[...]) * q_scale).astype(q_ref.dtype)
    k = _dot(hn, wkb[...])
    k_ref[...] = k
    kb_ref[...] = k.astype(kb_ref.dtype)
    v = _dot(hn, wvb[...])
    v_ref[...] = v
    vb_ref[...] = v.astype(vb_ref.dtype)


def _qkv_proj(hn, hs, w_in, col0, da, q_scale, qs_scale):
    m, d = hn.shape
    ms = hs.shape[0]
    tm = _pick(m, 1024)
    tn = _pick(da, 512)
    nseg = da // tn
    c0 = col0 // tn
    blk = pl.BlockSpec((tm, tn), lambda j, i: (i, j))
    srow = pl.BlockSpec((ms, tn), lambda j, i: (0, j))
    wspec = lambda off: _resident((d, tn), lambda j, i: (0, c0 + off + j))
    return pl.pallas_call(
        functools.partial(_qkv_kernel, q_scale=q_scale, qs_scale=qs_scale),
        out_shape=(jax.ShapeDtypeStruct((m, da), BF16),
                   jax.ShapeDtypeStruct((m, da), F32),
                   jax.ShapeDtypeStruct((m, da), F32),
                   jax.ShapeDtypeStruct((m, da), BF16),
                   jax.ShapeDtypeStruct((m, da), BF16),
                   jax.ShapeDtypeStruct((ms, da), F32),
                   jax.ShapeDtypeStruct((ms, da), F32),
                   jax.ShapeDtypeStruct((ms, da), F32)),
        grid=(nseg, m // tm),
        in_specs=[pl.BlockSpec((tm, d), lambda j, i: (i, 0)),
                  pl.BlockSpec((ms, d), lambda j, i: (0, 0)),
                  wspec(0), wspec(nseg), wspec(2 * nseg)],
        out_specs=(blk, blk, blk, blk, blk, srow, srow, srow),
        scratch_shapes=[pltpu.VMEM((d, tn), BF16)] * 3,
        compiler_params=_params("arbitrary", "arbitrary"),
        name="qkv_proj",
    )(hn, hs, w_in, w_in, w_in)


def _diff_lambda(lq1_ref, lk1_ref, lq2_ref, lk2_ref, lam_init):
    a = jnp.sum(lq1_ref[...] * lk1_ref[...], axis=-1, keepdims=True)
    b = jnp.sum(lq2_ref[...] * lk2_ref[...], axis=-1, keepdims=True)
    return jnp.exp(a) - jnp.exp(b) + lam_init


def _attn_prompt_kernel(lq1_ref, lk1_ref, lq2_ref, lk2_ref, g_ref, q_ref, k_ref, v_ref, o_ref,
                        s_ref, p_ref, c_ref, *, tq, rc, lam_init):
    seq = q_ref.shape[0]
    hd = q_ref.shape[1] // 2
    nq = seq // tq
    lam = _diff_lambda(lq1_ref, lk1_ref, lq2_ref, lk2_ref, lam_init)

    def scores(qi):
        slot = qi % 2
        lo, hi = qi * tq, (qi + 1) * tq
        q = q_ref[lo:hi, :]
        for mp in range(2):
            cols = slice(mp * hd, (mp + 1) * hd)
            if qi > 0:
                s_ref[slot, mp, :, :lo] = _dot_nt(q[:, cols], k_ref[:lo, cols])
            sd = _dot_nt(q[:, cols], k_ref[lo:hi, cols])
            keep = (lax.broadcasted_iota(jnp.int32, sd.shape, 1) <= lax.broadcasted_iota(jnp.int32, sd.shape, 0))
            s_ref[slot, mp, :, lo:hi] = jnp.where(keep, sd, MASK_VALUE)

    def softmax_diff(qi):
        slot = qi % 2
        hi = (qi + 1) * tq
        for r in range(tq // rc):
            rows = slice(r * rc, (r + 1) * rc)
            x1 = s_ref[slot, 0, rows, :hi]
            x2 = s_ref[slot, 1, rows, :hi]
            e1 = jnp.exp2(x1 - jnp.max(x1, axis=-1, keepdims=True))
            e2 = jnp.exp2(x2 - jnp.max(x2, axis=-1, keepdims=True))
            l1 = jnp.sum(e1, axis=-1, keepdims=True)
            l2 = jnp.sum(e2, axis=-1, keepdims=True)
            p_ref[slot, rows, :hi] = (e1 - e2 * (lam * l1 / l2)).astype(p_ref.dtype)
            c_ref[slot, rows, :] = 1.0 / l1

    def values(qi):
        slot = qi % 2
        lo, hi = qi * tq, (qi + 1) * tq
        o = _dot(p_ref[slot, :, :hi], v_ref[:hi, :]) * c_ref[slot]
        o_ref[lo:hi, :] = (_rms(o, g_ref[...]) * (1.0 - lam_init)).astype(o_ref.dtype)

    scores(0)
    for qi in range(nq):
        if qi + 1 < nq:
            scores(qi + 1)
        softmax_diff(qi)
        values(qi)


def _attn_prompt(q, kb, vb, lam_refs, subln_g, batch, seq, n_heads, lam_init):
    m, da = q.shape
    hd2 = da // n_heads
    tq = _pick(seq, 256)
    rc = 32
    vec = lambda n: pl.BlockSpec((1, n), lambda b, h: (0, 0))
    head = pl.BlockSpec((seq, hd2), lambda b, h: (b, h))
    return pl.pallas_call(
        functools.partial(_attn_prompt_kernel, tq=tq, rc=rc, lam_init=lam_init),
        out_shape=jax.ShapeDtypeStruct((m, da), BF16),
        grid=(batch, n_heads),
        in_specs=[vec(hd2 // 2)] * 4 + [vec(hd2), head, head, head],
        out_specs=head,
        scratch_shapes=[pltpu.VMEM((2, 2, tq, seq), F32), pltpu.VMEM((2, tq, seq), BF16),
                        pltpu.VMEM((2, tq, 1), F32)],
        compiler_params=_params("parallel", "parallel"),
        name="diff_attn_prompt",
    )(*lam_refs, subln_g, q, kb, vb)


def _attn_sample_tc_kernel(pt_ref, q2_ref, kn_ref, vn_ref, *rest, pages_per_step):
    del pt_ref
    k_refs = rest[:pages_per_step]
    v_refs = rest[pages_per_step:2 * pages_per_step]
    m_ref, l_ref, acc1_ref, acc2_ref = (r.at[0] for r in rest[2 * pages_per_step:])
    j = pl.program_id(1)
    n_heads, hd2 = kn_ref.shape[-2:]
    q2 = q2_ref[...]
    lane = lax.broadcasted_iota(jnp.int32, (n_heads, LANES), 1)
    head = lax.broadcasted_iota(jnp.int32, (n_heads, LANES), 0)
    own = (lane == head, lane == head + n_heads)
    ones = jnp.ones((LANES, LANES), BF16)

    def spread(x, mp):
        lead = x.shape[:-2]
        sel = jnp.where(own[mp], x, 0.0).astype(BF16).reshape((-1, LANES))
        return _dot(sel, ones).reshape(lead + (n_heads, LANES))

    def spread_f32(x, mp):
        hi = x.astype(BF16).astype(F32)
        return spread(hi, mp) + spread(x - hi, mp)

    def tile2(x):
        return jnp.concatenate([x, x], axis=-1)

    @pl.when(j == 0)
    def _():
        m_ref[...] = _dot(kn_ref[0].astype(BF16), q2)
        l_ref[...] = jnp.ones_like(l_ref)
        acc1_ref[...] = vn_ref[0]
        acc2_ref[...] = vn_ref[0]

    page = k_refs[0].shape[0]
    ss = [_dot(k_ref[...].reshape(page * n_heads, hd2).astype(BF16), q2).reshape(page, n_heads, LANES)
          for k_ref in k_refs]
    m_old = m_ref[...]
    m_new = m_old
    for s in ss:
        m_new = jnp.maximum(m_new, jnp.max(s, axis=0))
    alpha = jnp.exp2(m_old - m_new)
    ps = [jnp.exp2(s - m_new[None]) for s in ss]
    l_new = alpha * l_ref[...]
    for p in ps:
        l_new = l_new + jnp.sum(p, axis=0)
    l_ref[...] = l_new
    m_ref[...] = m_new
    for mp, acc_ref in enumerate((acc1_ref, acc2_ref)):
        acc = tile2(spread_f32(alpha, mp)) * acc_ref[...]
        for p, v_ref in zip(ps, v_refs):
            acc = acc + jnp.sum(tile2(spread(p, mp)) * v_ref[...], axis=0)
        acc_ref[...] = acc


def _attn_sample_tc(q, k_new, v_new, cache_k, cache_v, page_table, layer, first_page):
    db, n_heads, hd2 = q.shape
    hd = hd2 // 2
    page = cache_k.shape[2]
    n_pages = page_table.shape[1] - first_page
    pps = 8 if n_pages % 8 == 0 else 1
    assert 2 * n_heads <= LANES and n_pages > 0
    qt = (q * LOG2_E).reshape(db, n_heads, 2, hd).transpose(0, 2, 3, 1)
    q2 = jnp.zeros((db, 2, hd, 2, n_heads), q.dtype)
    q2 = q2.at[:, 0, :, 0, :].set(qt[:, 0]).at[:, 1, :, 1, :].set(qt[:, 1])
    q2 = jnp.pad(q2.reshape(db, hd2, 2 * n_heads), ((0, 0), (0, 0), (0, LANES - 2 * n_heads))).astype(BF16)
    tok = pl.BlockSpec((1, n_heads, hd2), lambda b, j, pt: (b, 0, 0))
    stat = pl.BlockSpec((1, n_heads, LANES), lambda b, j, pt: (b, 0, 0))

    def page_spec(p):
        return pl.BlockSpec((None, None, page, n_heads, hd2),
                            lambda b, j, pt: (layer, pt[b, first_page + j * pps + p], 0, 0, 0))

    grid_spec = pltpu.PrefetchScalarGridSpec(
        num_scalar_prefetch=1,
        grid=(db, n_pages // pps),
        in_specs=[pl.BlockSpec((None, hd2, LANES), lambda b, j, pt: (b, 0, 0)), tok, tok]
                 + [page_spec(p) for p in range(pps)] * 2,
        out_specs=(stat, stat, tok, tok),
    )
    return pl.pallas_call(
        functools.partial(_attn_sample_tc_kernel, pages_per_step=pps),
        out_shape=(jax.ShapeDtypeStruct((db, n_heads, LANES), F32), jax.ShapeDtypeStruct((db, n_heads, LANES), F32),
                   jax.ShapeDtypeStruct((db, n_heads, hd2), F32), jax.ShapeDtypeStruct((db, n_heads, hd2), F32)),
        grid_spec=grid_spec,
        compiler_params=_params("parallel", "arbitrary"),
        name="diff_attn_sample_tc",
    )(page_table, q2, k_new, v_new, *([cache_k] * pps), *([cache_v] * pps))


def _attn_sample_sc(q, cache_k, cache_v, page_table, layer, n_pages):
    db, n_heads, hd2 = q.shape
    hd = hd2 // 2
    depth, nphys, page = cache_k.shape[:3]
    cpp = page // SC_CHUNK
    n_chunks = n_pages * cpp
    assert db == SC_WORKERS and page % SC_CHUNK == 0 and n_chunks % 2 == 0 and page_table.shape[1] % SC_LANES == 0
    nv = hd // SC_LANES
    ck = cache_k.reshape(depth * nphys * page, n_heads, hd2)
    cv = cache_v.reshape(depth * nphys * page, n_heads, hd2)
    mesh = plsc.VectorSubcoreMesh(core_axis_name="c", subcore_axis_name="s",
                                  num_cores=SC_CORES, num_subcores=SC_SUBCORES)
    buf = pltpu.VMEM((SC_CHUNK, n_heads, hd2), F32)
    idx_t = pltpu.VMEM((SC_LANES,), jnp.int32)
    sem = pltpu.SemaphoreType.DMA

    @functools.partial(
        pl.kernel, mesh=mesh, name="diff_attn_sample_sc",
        compiler_params=pltpu.CompilerParams(use_tc_tiling_on_sc=True, needs_layout_passes=False),
        out_type=(jax.ShapeDtypeStruct((db, 2, n_heads, hd2), F32),
                  jax.ShapeDtypeStruct((db, 2 * n_heads, SC_LANES), F32),
                  jax.ShapeDtypeStruct((db, 2 * n_heads, SC_LANES), F32)),
        scratch_types=[buf, buf, buf, buf, pltpu.VMEM((n_heads, hd2), F32), pltpu.VMEM((2, n_heads, hd2), F32),
                       pltpu.VMEM((2 * n_heads, SC_LANES), F32), pltpu.VMEM((2 * n_heads, SC_LANES), F32),
                       pltpu.VMEM((page_table.shape[1],), jnp.int32), idx_t, idx_t, sem, sem, sem, sem],
    )
    def sc_kernel(q_hbm, ck_hbm, cv_hbm, pt_hbm, acc_hbm, m_hbm, l_hbm,
                  k0, k1, v0, v1, q_v, acc_v, m_v, l_v, pt_v, idx0, idx1, sk0, sk1, sv0, sv1):
        b = lax.axis_index("s") * SC_CORES + lax.axis_index("c")
        pltpu.sync_copy(pt_hbm.at[b], pt_v)
        pltpu.sync_copy(q_hbm.at[b], q_v)
        lanes = lax.iota(jnp.int32, SC_LANES)
        zero = jnp.zeros((SC_LANES,), F32)
        for r in range(2 * n_heads):
            m_v[r, :] = jnp.full((SC_LANES,), -jnp.inf, F32)
            l_v[r, :] = zero
        for mp in range(2):
            for h in range(n_heads):
                for i in range(hd2 // SC_LANES):
                    acc_v[mp, h, pl.ds(i * SC_LANES, SC_LANES)] = zero
        slots = ((k0, v0, idx0, sk0, sv0), (k1, v1, idx1, sk1, sv1))

        def copies(slot):
            kb, vb, idx, sk, sv = slots[slot]
            rows = idx.at[pl.ds(0, SC_CHUNK)]
            return (pltpu.make_async_copy(ck_hbm.at[rows], kb, sk), pltpu.make_async_copy(cv_hbm.at[rows], vb, sv))

        def issue(c, slot):
            j = c // cpp
            ptv = pt_v[pl.ds(pl.multiple_of((j // SC_LANES) * SC_LANES, SC_LANES), SC_LANES)]
            pg = ptv.at[jnp.full((SC_LANES,), j % SC_LANES, jnp.int32)].get(mode="promise_in_bounds")
            slots[slot][2][...] = (layer * nphys + pg) * page + (c % cpp) * SC_CHUNK + lanes
            for cpy in copies(slot):
                cpy.start()

        def wait(slot):
            for cpy in copies(slot):
                cpy.wait()

        def fold(slot):
            kb, vb = slots[slot][:2]

            @pl.loop(0, n_heads)
            def _(h):
                ps = []
                alphas = []
                for mp in range(2):
                    qs = [q_v[h, pl.ds(mp * hd + i * SC_LANES, SC_LANES)] for i in range(nv)]
                    svec = jnp.full((SC_LANES,), -jnp.inf, F32)
                    for t in range(SC_CHUNK):
                        a = qs[0] * kb[t, h, pl.ds(mp * hd, SC_LANES)]
                        for i in range(1, nv):
                            a = a + qs[i] * kb[t, h, pl.ds(mp * hd + i * SC_LANES, SC_LANES)]
                        svec = jnp.where(lanes == t, jnp.sum(a), svec)
                    row = mp * n_heads + h
                    m_old = m_v[row, :]
                    m_new = jnp.maximum(m_old, jnp.max(svec))
                    alpha = jnp.exp(m_old - m_new)
                    p = jnp.exp(svec - m_new)
                    l_v[row, :] = alpha * l_v[row, :] + jnp.sum(p)
                    m_v[row, :] = m_new
                    ps.append(p)
                    alphas.append(alpha)
                for i in range(hd2 // SC_LANES):
                    cols = pl.ds(i * SC_LANES, SC_LANES)
                    a1 = alphas[0] * acc_v[0, h, cols]
                    a2 = alphas[1] * acc_v[1, h, cols]
                    for t in range(SC_CHUNK):
                        v = vb[t, h, cols]
                        a1 = a1 + ps[0][t] * v
                        a2 = a2 + ps[1][t] * v
                    acc_v[0, h, cols] = a1
                    acc_v[1, h, cols] = a2

        issue(0, 0)

        @pl.loop(0, n_chunks, step=2)
        def _(c):
            issue(c + 1, 1)
            wait(0)
            fold(0)

            @pl.when(c + 2 < n_chunks)
            def _():
                issue(c + 2, 0)

            wait(1)
            fold(1)

        pltpu.sync_copy(acc_v, acc_hbm.at[b])
        pltpu.sync_copy(m_v, m_hbm.at[b])
        pltpu.sync_copy(l_v, l_hbm.at[b])

    return sc_kernel(q, ck, cv, page_table)


def _attn_finish_kernel(lq1_ref, lk1_ref, lq2_ref, lk2_ref, g_ref, *refs, lam_init, n_sc):
    o_ref = refs[-1]
    lam = _diff_lambda(lq1_ref, lk1_ref, lq2_ref, lk2_ref, lam_init)
    outs = []
    for mp in range(2):
        if n_sc:
            q_ref, kn_ref, vn_ref, as_ref, ms_ref, ls_ref = refs[:-1]
            n_heads, hd2 = q_ref.shape[1:]
            cols = slice(mp * hd2 // 2, (mp + 1) * hd2 // 2)
            rows = slice(mp * n_heads, (mp + 1) * n_heads)
            m_t = jnp.sum(q_ref[:, :, cols] * kn_ref[:, :, cols], axis=-1, keepdims=True) * LOG2_E
            m_s = ms_ref[:, rows, 0:1] * LOG2_E
            m = jnp.maximum(m_t, m_s)
            w_t = jnp.exp2(m_t - m)
            w_s = jnp.exp2(m_s - m)
            outs.append((w_t * vn_ref[...] + w_s * as_ref[:, mp]) / (w_t + w_s * ls_ref[:, rows, 0:1]))
        else:
            mt_ref, lt_ref, a1_ref, a2_ref = refs[:-1]
            n_heads = mt_ref.shape[1]
            lane = lax.broadcasted_iota(jnp.int32, mt_ref.shape, 2)
            head = lax.broadcasted_iota(jnp.int32, mt_ref.shape, 1)
            l_t = jnp.sum(jnp.where(lane == head + mp * n_heads, lt_ref[...], 0.0), axis=-1, keepdims=True)
            outs.append((a1_ref, a2_ref)[mp][...] / l_t)
    o = outs[0] - lam * outs[1]
    o_ref[...] = (_rms(o, g_ref[...]) * (1.0 - lam_init)).astype(o_ref.dtype)


def _attn_finish(state, lam_refs, subln_g, lam_init, n_sc):
    return pl.pallas_call(
        functools.partial(_attn_finish_kernel, lam_init=lam_init, n_sc=n_sc),
        out_shape=jax.ShapeDtypeStruct(state[2].shape, BF16),
        compiler_params=_params(),
        name="diff_attn_sample_finish",
    )(*lam_refs, subln_g, *state)


def _merge_kernel(ya_ref, ob_ref, ga_ref, gb_ref, wc_ref, wa_ref, m_ref, wcb, wab):
    _cast_weights(pl.program_id(1) == 0, ((wc_ref, wcb), (wa_ref, wab)))
    y_a = _dot(ya_ref[...], wcb[...])
    y_b = _dot(ob_ref[...], wab[...])
    m_ref[...] = (ga_ref[...].astype(F32) * y_a + gb_ref[...].astype(F32) * y_b).astype(m_ref.dtype)


def _merge(ya, ob, ga, gb, w_out_conv, w_out_attn):
    m = ya.shape[0]
    dc, dm = w_out_conv.shape
    da = w_out_attn.shape[0]
    tm = _pick(m, 1024)
    tn = _pick(dm, 512)
    row = lambda k: pl.BlockSpec((tm, k), lambda j, i: (i, 0))
    blk = pl.BlockSpec((tm, tn), lambda j, i: (i, j))
    wspec = lambda k: _resident((k, tn), lambda j, i: (0, j))
    return pl.pallas_call(
        _merge_kernel,
        out_shape=jax.ShapeDtypeStruct((m, dm), BF16),
        grid=(dm // tn, m // tm),
        in_specs=[row(dc), row(da), blk, blk, wspec(dc), wspec(da)],
        out_specs=blk,
        scratch_shapes=[pltpu.VMEM((dc, tn), BF16), pltpu.VMEM((da, tn), BF16)],
        compiler_params=_params("arbitrary", "arbitrary"),
        name="gated_merge",
    )(ya, ob, ga, gb, w_out_conv, w_out_attn)


def _oproj_kernel(m_ref, x_ref, wo_ref, g_ref, h_ref, f_ref, wob):
    _cast_weights(pl.program_id(0) == 0, ((wo_ref, wob),))
    h = x_ref[...] + _dot(m_ref[...], wob[...])
    h_ref[...] = h
    f_ref[...] = _rms(h, g_ref[...]).astype(f_ref.dtype)


def _oproj_norm(mg, x, w_o, g):
    m, d = x.shape
    dm = mg.shape[1]
    tm = _pick(m, 512)
    rows = lambda c: pl.BlockSpec((tm, c), lambda i: (i, 0))
    return pl.pallas_call(
        _oproj_kernel,
        out_shape=(jax.ShapeDtypeStruct((m, d), F32), jax.ShapeDtypeStruct((m, d), BF16)),
        grid=(m // tm,),
        in_specs=[rows(dm), rows(d), _resident(w_o.shape, lambda i: (0, 0)), pl.BlockSpec((1, d), lambda i: (0, 0))],
        out_specs=(rows(d), rows(d)),
        scratch_shapes=[pltpu.VMEM(w_o.shape, BF16)],
        compiler_params=_params("arbitrary"),
        name="oproj_norm",
    )(mg, x, w_o, g.reshape(1, d))


def _ffn_kernel(f_ref, h_ref, wg_ref, wu_ref, wd_ref, g_ref, y_ref, acc_ref, *, final_norm):
    c = pl.program_id(1)

    @pl.when(c == 0)
    def _():
        acc_ref[...] = h_ref[...]

    f = f_ref[...]
    a = jax.nn.silu(_dot(f, wg_ref[...])) * _dot(f, wu_ref[...])
    acc_ref[...] += _dot(a.astype(wd_ref.dtype), wd_ref[...])

    @pl.when(c == pl.num_programs(1) - 1)
    def _():
        y_ref[...] = _rms(acc_ref[...], g_ref[...]) if final_norm else acc_ref[...]


def _ffn(f, h, w_gate, w_up, w_down, g_final, final_norm):
    m, d = h.shape
    dff = w_gate.shape[1]
    tm = _pick(m, 512)
    tc = _pick(dff, 512)
    rows = pl.BlockSpec((tm, d), lambda i, c: (i, 0))
    return pl.pallas_call(
        functools.partial(_ffn_kernel, final_norm=final_norm),
        out_shape=jax.ShapeDtypeStruct((m, d), F32),
        grid=(m // tm, dff // tc),
        in_specs=[rows, rows,
                  pl.BlockSpec((d, tc), lambda i, c: (0, c)),
                  pl.BlockSpec((d, tc), lambda i, c: (0, c)),
                  pl.BlockSpec((tc, d), lambda i, c: (c, 0)),
                  pl.BlockSpec((1, d), lambda i, c: (0, 0))],
        out_specs=rows,
        scratch_shapes=[pltpu.VMEM((tm, d), F32)],
        compiler_params=_params("parallel", "arbitrary"),
        name="ffn",
    )(f, h, w_gate, w_up, w_down, g_final.reshape(1, d))


def kernel(x_prompt, x_sample, cache_k, cache_v, state_conv, page_table, w_in, conv_w, w_out_conv, lambda_q1, lambda_k1, lambda_q2, lambda_k2, subln_g, w_out_attn, w_o, norm_mix_g, norm_ffn_g, w_gate, w_up, w_down, norm_final_g):
    batch, seq, d = x_prompt.shape
    db, dseq, _ = x_sample.shape
    assert dseq == 1, "decode rows carry one new token per sequence"
    depth = w_in.shape[0]
    n_heads, hd2 = cache_k.shape[-2:]
    hd = hd2 // 2
    da = n_heads * hd2
    dc = conv_w.shape[1]
    assert state_conv.shape[2] == CONV_WIDTH - 1 and conv_w.shape[2] == CONV_WIDTH
    scale = hd ** -0.5
    qkv_col0 = 3 * dc
    gate_col0 = 3 * dc + 3 * da
    n_pages = page_table.shape[1]
    on_sc = db == SC_WORKERS and n_pages % SC_LANES == 0 and cache_k.shape[2] % SC_CHUNK == 0 and hd % SC_LANES == 0

    xp = x_prompt.reshape(batch * seq, d)
    xs = x_sample.reshape(db, d)
    outs = [[] for _ in range(6)]
    for l in range(depth):
        lam_init = 0.8 - 0.6 * math.exp(-0.3 * l)
        w_g_b = w_gate[l].astype(BF16)
        w_u_b = w_up[l].astype(BF16)
        w_d_b = w_down[l].astype(BF16)
        cw_t = conv_w[l].T
        lam_refs = [a[l].reshape(1, hd) for a in (lambda_q1, lambda_k1, lambda_q2, lambda_k2)]
        sub_g = subln_g[l].reshape(1, hd2)
        st = state_conv[l]

        hn, hs = _norm_cast(xp, xs, norm_mix_g[l])
        q, k32, v32, kb, vb, q_s, k_s, v_s = _qkv_proj(hn, hs, w_in[l], qkv_col0, da, scale * LOG2_E, scale)
        to_heads = lambda a: a.reshape(db, n_heads, hd2)
        q_s3, k_s3, v_s3 = to_heads(q_s), to_heads(k_s), to_heads(v_s)
        if on_sc:
            state = (q_s3, k_s3, v_s3) + _attn_sample_sc(q_s3, cache_k, cache_v, page_table, l, n_pages)
        else:
            state = _attn_sample_tc(q_s3, k_s3, v_s3, cache_k, cache_v, page_table, l, 0)
        ya, ga, gb, conv_p, ya_s, ga_s, gb_s, u_s = _conv_inproj(hn, hs, st[:, 1, :], st[:, 0, :], w_in[l], cw_t,
                                                                 gate_col0, batch, seq)
        ob = _attn_prompt(q, kb, vb, lam_refs, sub_g, batch, seq, n_heads, lam_init)
        mg = _merge(ya, ob, ga, gb, w_out_conv[l], w_out_attn[l])
        h, f = _oproj_norm(mg, xp, w_o[l], norm_ffn_g[l])
        xp = _ffn(f, h, w_g_b, w_u_b, w_d_b, norm_final_g, l == depth - 1)
        ob_s = _attn_finish(state, lam_refs, sub_g, lam_init, n_pages if on_sc else 0)
        mg_s = _merge(ya_s, ob_s.reshape(db, da), ga_s, gb_s, w_out_conv[l], w_out_attn[l])
        h_s, f_s = _oproj_norm(mg_s, xs, w_o[l], norm_ffn_g[l])
        xs = _ffn(f_s, h_s, w_g_b, w_u_b, w_d_b, norm_final_g, l == depth - 1)

        outs[0].append(k32.reshape(batch, seq, n_heads, hd2))
        outs[1].append(v32.reshape(batch, seq, n_heads, hd2))
        outs[2].append(conv_p)
        outs[3].append(k_s.reshape(db, 1, n_heads, hd2))
        outs[4].append(v_s.reshape(db, 1, n_heads, hd2))
        outs[5].append(jnp.stack([st[:, 1, :], u_s], axis=1))

    y_prompt = xp.reshape(batch, seq, d)
    y_sample = xs.reshape(db, 1, d)
    return (y_prompt, y_sample) + tuple(jnp.stack(o) for o in outs)
```

```python
import functools
import math

import jax
import jax.numpy as jnp
from jax import lax
from jax.experimental import pallas as pl
from jax.experimental.pallas import tpu as pltpu
from jax.experimental.pallas import tpu_sc as plsc

F32 = jnp.float32
BF16 = jnp.bfloat16
EPS = 1e-6
MASK_VALUE = -1e30
CONV_WIDTH = 3
LOG2_E = math.log2(math.e)
V7X_VMEM_LIMIT_BYTES = 60 * 1024 * 1024
LANES = 128
SC_LANES = 16
SC_CORES = 2
SC_SUBCORES = 16
SC_WORKERS = SC_CORES * SC_SUBCORES
SC_CHUNK = 8


def _pick(n, pref):
    t = pref
    while t >= LANES:
        if n % t == 0:
            return t
        t //= 2
    return n


def _params(*sem):
    return pltpu.CompilerParams(dimension_semantics=sem, vmem_limit_bytes=V7X_VMEM_LIMIT_BYTES)


def _rms(x, g):
    return x * lax.rsqrt(jnp.mean(x * x, axis=-1, keepdims=True) + EPS) * g


def _dot(a, b):
    return jnp.dot(a, b, preferred_element_type=F32)


def _dot_nt(a, b):
    return lax.dot_general(a, b, (((1,), (1,)), ((), ())), preferred_element_type=F32)


def _resident(shape, index_map):
    return pl.BlockSpec(shape, index_map, pipeline_mode=pl.Buffered(1))


def _cast_weights(first_step, pairs):
    @pl.when(first_step)
    def _():
        for src, dst in pairs:
            dst[...] = src[...].astype(dst.dtype)


def _norm_kernel(x_ref, xs_ref, g_ref, o_ref, os_ref):
    o_ref[...] = _rms(x_ref[...], g_ref[...]).astype(o_ref.dtype)

    @pl.when(pl.program_id(0) == 0)
    def _():
        os_ref[...] = _rms(xs_ref[...], g_ref[...]).astype(os_ref.dtype)


def _norm_cast(x, xs, g):
    m, d = x.shape
    ms = xs.shape[0]
    tm = _pick(m, 512)
    return pl.pallas_call(
        _norm_kernel,
        out_shape=(jax.ShapeDtypeStruct((m, d), BF16), jax.ShapeDtypeStruct((ms, d), BF16)),
        grid=(m // tm,),
        in_specs=[pl.BlockSpec((tm, d), lambda i: (i, 0)),
                  pl.BlockSpec((ms, d), lambda i: (0, 0)),
                  pl.BlockSpec((1, d), lambda i: (0, 0))],
        out_specs=(pl.BlockSpec((tm, d), lambda i: (i, 0)),
                   pl.BlockSpec((ms, d), lambda i: (0, 0))),
        compiler_params=_params("arbitrary"),
        name="norm_cast",
    )(x, xs, g.reshape(1, d))


def _conv_taps(u, um1, um2, cw_ref):
    return um2 * cw_ref[0:1, :] + um1 * cw_ref[1:2, :] + u * cw_ref[2:3, :]


def _conv_kernel(hn_ref, hs_ref, um1_ref, um2_ref, wb_ref, wc_ref, wh_ref, wga_ref, wgb_ref, cw_ref,
                 ya_ref, ga_ref, gb_ref, tail_ref, yas_ref, gas_ref, gbs_ref, us_ref,
                 wbb, wcb, whb, wgab, wgbb, carry_ref, *, tiles_per_seq):
    i = pl.program_id(1)
    _cast_weights(i == 0, ((wb_ref, wbb), (wc_ref, wcb), (wh_ref, whb), (wga_ref, wgab), (wgb_ref, wgbb)))

    @pl.when(i == 0)
    def _():
        hs = hs_ref[...]
        us = _dot(hs, wcb[...]) * _dot(hs, whb[...])
        yas_ref[...] = (_dot(hs, wbb[...]) * _conv_taps(us, um1_ref[...], um2_ref[...], cw_ref)).astype(yas_ref.dtype)
        us_ref[...] = us
        gas_ref[...] = jax.nn.sigmoid(_dot(hs, wgab[...])).astype(gas_ref.dtype)
        gbs_ref[...] = jax.nn.sigmoid(_dot(hs, wgbb[...])).astype(gbs_ref.dtype)

    @pl.when(i % tiles_per_seq == 0)
    def _():
        carry_ref[...] = jnp.zeros_like(carry_ref)

    hn = hn_ref[...]
    u = _dot(hn, wcb[...]) * _dot(hn, whb[...])
    tm = u.shape[0]
    row = lax.broadcasted_iota(jnp.int32, u.shape, 0)
    c0 = carry_ref[0:1, :]
    c1 = carry_ref[1:2, :]
    um1 = jnp.where(row == 0, c1, pltpu.roll(u, 1, 0))
    um2 = jnp.where(row == 0, c0, jnp.where(row == 1, c1, pltpu.roll(u, 2, 0)))
    yc = _conv_taps(u, um1, um2, cw_ref)
    ya_ref[...] = (_dot(hn, wbb[...]) * yc).astype(ya_ref.dtype)
    last = u[tm - (CONV_WIDTH - 1):, :]
    carry_ref[...] = last
    tail_ref[0] = last
    ga_ref[...] = jax.nn.sigmoid(_dot(hn, wgab[...])).astype(ga_ref.dtype)
    gb_ref[...] = jax.nn.sigmoid(_dot(hn, wgbb[...])).astype(gb_ref.dtype)


def _conv_inproj(hn, hs, um1, um2, w_in, cw_t, gate_col0, batch, seq):
    m, d = hn.shape
    ms = hs.shape[0]
    dc = cw_t.shape[1]
    tm = _pick(seq, 1024)
    tn = _pick(dc, 512)
    nseg = dc // tn
    g0 = gate_col0 // tn
    wspec = lambda off: _resident((d, tn), lambda j, i: (0, off + j))
    blk = pl.BlockSpec((tm, tn), lambda j, i: (i, j))
    srow = pl.BlockSpec((ms, tn), lambda j, i: (0, j))
    act = lambda rows: jax.ShapeDtypeStruct((rows, dc), BF16)
    return pl.pallas_call(
        functools.partial(_conv_kernel, tiles_per_seq=seq // tm),
        out_shape=(act(m), act(m), act(m),
                   jax.ShapeDtypeStruct((batch, CONV_WIDTH - 1, dc), F32),
                   act(ms), act(ms), act(ms),
                   jax.ShapeDtypeStruct((ms, dc), F32)),
        grid=(nseg, m // tm),
        in_specs=[pl.BlockSpec((tm, d), lambda j, i: (i, 0)),
                  pl.BlockSpec((ms, d), lambda j, i: (0, 0)),
                  srow, srow,
                  wspec(0), wspec(nseg), wspec(2 * nseg), wspec(g0), wspec(g0 + nseg),
                  pl.BlockSpec((CONV_WIDTH, tn), lambda j, i: (0, j))],
        out_specs=(blk, blk, blk,
                   pl.BlockSpec((1, CONV_WIDTH - 1, tn), lambda j, i: (i // (seq // tm), 0, j)),
                   srow, srow, srow, srow),
        scratch_shapes=[pltpu.VMEM((d, tn), BF16)] * 5 + [pltpu.VMEM((CONV_WIDTH - 1, tn), F32)],
        compiler_params=_params("arbitrary", "arbitrary"),
        name="conv_inproj",
    )(hn, hs, um1, um2, w_in, w_in, w_in, w_in, w_in, cw_t)


def _qkv_kernel(hn_ref, hs_ref, wq_ref, wk_ref, wv_ref, q_ref, k_ref, v_ref, kb_ref, vb_ref,
                qs_ref, ks_ref, vs_ref, wqb, wkb, wvb, *, q_scale, qs_scale):
    i = pl.program_id(1)
    _cast_weights(i == 0, ((wq_ref, wqb), (wk_ref, wkb), (wv_ref, wvb)))

    @pl.when(i == 0)
    def _():
        hs = hs_ref[...]
        qs_ref[...] = _dot(hs, wqb[...]) * qs_scale
        ks_ref[...] = _dot(hs, wkb[...])
        vs_ref[...] = _dot(hs, wvb[...])

    hn = hn_ref[...]
    q_ref[...] = (_dot(hn, wqb[...]) * q_scale).astype(q_ref.dtype)
    k = _dot(hn, wkb[...])
    k_ref[...] = k
    kb_ref[...] = k.astype(kb_ref.dtype)
    v = _dot(hn, wvb[...])
    v_ref[...] = v
    vb_ref[...] = v.astype(vb_ref.dtype)


def _qkv_proj(hn, hs, w_in, col0, da, q_scale, qs_scale):
    m, d = hn.shape
    ms = hs.shape[0]
    tm = _pick(m, 1024)
    tn = _pick(da, 512)
    nseg = da // tn
    c0 = col0 // tn
    blk = pl.BlockSpec((tm, tn), lambda j, i: (i, j))
    srow = pl.BlockSpec((ms, tn), lambda j, i: (0, j))
    wspec = lambda off: _resident((d, tn), lambda j, i: (0, c0 + off + j))
    return pl.pallas_call(
        functools.partial(_qkv_kernel, q_scale=q_scale, qs_scale=qs_scale),
        out_shape=(jax.ShapeDtypeStruct((m, da), BF16),
                   jax.ShapeDtypeStruct((m, da), F32),
                   jax.ShapeDtypeStruct((m, da), F32),
                   jax.ShapeDtypeStruct((m, da), BF16),
                   jax.ShapeDtypeStruct((m, da), BF16),
                   jax.ShapeDtypeStruct((ms, da), F32),
                   jax.ShapeDtypeStruct((ms, da), F32),
                   jax.ShapeDtypeStruct((ms, da), F32)),
        grid=(nseg, m // tm),
        in_specs=[pl.BlockSpec((tm, d), lambda j, i: (i, 0)),
                  pl.BlockSpec((ms, d), lambda j, i: (0, 0)),
                  wspec(0), wspec(nseg), wspec(2 * nseg)],
        out_specs=(blk, blk, blk, blk, blk, srow, srow, srow),
        scratch_shapes=[pltpu.VMEM((d, tn), BF16)] * 3,
        compiler_params=_params("arbitrary", "arbitrary"),
        name="qkv_proj",
    )(hn, hs, w_in, w_in, w_in)


def _diff_lambda(lq1_ref, lk1_ref, lq2_ref, lk2_ref, lam_init):
    a = jnp.sum(lq1_ref[...] * lk1_ref[...], axis=-1, keepdims=True)
    b = jnp.sum(lq2_ref[...] * lk2_ref[...], axis=-1, keepdims=True)
    return jnp.exp(a) - jnp.exp(b) + lam_init


def _attn_prompt_kernel(lq1_ref, lk1_ref, lq2_ref, lk2_ref, g_ref, q_ref, k_ref, v_ref, o_ref,
                        s_ref, p_ref, c_ref, *, tq, rc, lam_init):
    seq = q_ref.shape[0]
    hd = q_ref.shape[1] // 2
    nq = seq // tq
    lam = _diff_lambda(lq1_ref, lk1_ref, lq2_ref, lk2_ref, lam_init)

    def scores(qi):
        slot = qi % 2
        lo, hi = qi * tq, (qi + 1) * tq
        q = q_ref[lo:hi, :]
        for mp in range(2):
            cols = slice(mp * hd, (mp + 1) * hd)
            if qi > 0:
                s_ref[slot, mp, :, :lo] = _dot_nt(q[:, cols], k_ref[:lo, cols])
            sd = _dot_nt(q[:, cols], k_ref[lo:hi, cols])
            keep = (lax.broadcasted_iota(jnp.int32, sd.shape, 1) <= lax.broadcasted_iota(jnp.int32, sd.shape, 0))
            s_ref[slot, mp, :, lo:hi] = jnp.where(keep, sd, MASK_VALUE)

    def softmax_diff(qi):
        slot = qi % 2
        hi = (qi + 1) * tq
        for r in range(tq // rc):
            rows = slice(r * rc, (r + 1) * rc)
            x1 = s_ref[slot, 0, rows, :hi]
            x2 = s_ref[slot, 1, rows, :hi]
            e1 = jnp.exp2(x1 - jnp.max(x1, axis=-1, keepdims=True))
            e2 = jnp.exp2(x2 - jnp.max(x2, axis=-1, keepdims=True))
            l1 = jnp.sum(e1, axis=-1, keepdims=True)
            l2 = jnp.sum(e2, axis=-1, keepdims=True)
            p_ref[slot, rows, :hi] = (e1 - e2 * (lam * l1 / l2)).astype(p_ref.dtype)
            c_ref[slot, rows, :] = 1.0 / l1

    def values(qi):
        slot = qi % 2
        lo, hi = qi * tq, (qi + 1) * tq
        o = _dot(p_ref[slot, :, :hi], v_ref[:hi, :]) * c_ref[slot]
        o_ref[lo:hi, :] = (_rms(o, g_ref[...]) * (1.0 - lam_init)).astype(o_ref.dtype)

    scores(0)
    for qi in range(nq):
        if qi + 1 < nq:
            scores(qi + 1)
        softmax_diff(qi)
        values(qi)


def _attn_prompt(q, kb, vb, lam_refs, subln_g, batch, seq, n_heads, lam_init):
    m, da = q.shape
    hd2 = da // n_heads
    tq = _pick(seq, 256)
    rc = 32
    vec = lambda n: pl.BlockSpec((1, n), lambda b, h: (0, 0))
    head = pl.BlockSpec((seq, hd2), lambda b, h: (b, h))
    return pl.pallas_call(
        functools.partial(_attn_prompt_kernel, tq=tq, rc=rc, lam_init=lam_init),
        out_shape=jax.ShapeDtypeStruct((m, da), BF16),
        grid=(batch, n_heads),
        in_specs=[vec(hd2 // 2)] * 4 + [vec(hd2), head, head, head],
        out_specs=head,
        scratch_shapes=[pltpu.VMEM((2, 2, tq, seq), F32), pltpu.VMEM((2, tq, seq), BF16),
                        pltpu.VMEM((2, tq, 1), F32)],
        compiler_params=_params("parallel", "parallel"),
        name="diff_attn_prompt",
    )(*lam_refs, subln_g, q, kb, vb)


def _attn_sample_tc_kernel(pt_ref, q2_ref, kn_ref, vn_ref, *rest, pages_per_step):
    del pt_ref
    k_refs = rest[:pages_per_step]
    v_refs = rest[pages_per_step:2 * pages_per_step]
    m_ref, l_ref, acc1_ref, acc2_ref = (r.at[0] for r in rest[2 * pages_per_step:])
    j = pl.program_id(1)
    n_heads, hd2 = kn_ref.shape[-2:]
    q2 = q2_ref[...]
    lane = lax.broadcasted_iota(jnp.int32, (n_heads, LANES), 1)
    head = lax.broadcasted_iota(jnp.int32, (n_heads, LANES), 0)
    own = (lane == head, lane == head + n_heads)
    ones = jnp.ones((LANES, LANES), BF16)

    def spread(x, mp):
        lead = x.shape[:-2]
        sel = jnp.where(own[mp], x, 0.0).astype(BF16).reshape((-1, LANES))
        return _dot(sel, ones).reshape(lead + (n_heads, LANES))

    def spread_f32(x, mp):
        hi = x.astype(BF16).astype(F32)
        return spread(hi, mp) + spread(x - hi, mp)

    def tile2(x):
        return jnp.concatenate([x, x], axis=-1)

    @pl.when(j == 0)
    def _():
        m_ref[...] = _dot(kn_ref[0].astype(BF16), q2)
        l_ref[...] = jnp.ones_like(l_ref)
        acc1_ref[...] = vn_ref[0]
        acc2_ref[...] = vn_ref[0]

    page = k_refs[0].shape[0]
    ss = [_dot(k_ref[...].reshape(page * n_heads, hd2).astype(BF16), q2).reshape(page, n_heads, LANES)
          for k_ref in k_refs]
    m_old = m_ref[...]
    m_new = m_old
    for s in ss:
        m_new = jnp.maximum(m_new, jnp.max(s, axis=0))
    alpha = jnp.exp2(m_old - m_new)
    ps = [jnp.exp2(s - m_new[None]) for s in ss]
    l_new = alpha * l_ref[...]
    for p in ps:
        l_new = l_new + jnp.sum(p, axis=0)
    l_ref[...] = l_new
    m_ref[...] = m_new
    for mp, acc_ref in enumerate((acc1_ref, acc2_ref)):
        acc = tile2(spread_f32(alpha, mp)) * acc_ref[...]
        for p, v_ref in zip(ps, v_refs):
            acc = acc + jnp.sum(tile2(spread(p, mp)) * v_ref[...], axis=0)
        acc_ref[...] = acc


def _attn_sample_tc(q, k_new, v_new, cache_k, cache_v, page_table, layer, first_page):
    db, n_heads, hd2 = q.shape
    hd = hd2 // 2
    page = cache_k.shape[2]
    n_pages = page_table.shape[1] - first_page
    pps = 8 if n_pages % 8 == 0 else 1
    assert 2 * n_heads <= LANES and n_pages > 0
    qt = (q * LOG2_E).reshape(db, n_heads, 2, hd).transpose(0, 2, 3, 1)
    q2 = jnp.zeros((db, 2, hd, 2, n_heads), q.dtype)
    q2 = q2.at[:, 0, :, 0, :].set(qt[:, 0]).at[:, 1, :, 1, :].set(qt[:, 1])
    q2 = jnp.pad(q2.reshape(db, hd2, 2 * n_heads), ((0, 0), (0, 0), (0, LANES - 2 * n_heads))).astype(BF16)
    tok = pl.BlockSpec((1, n_heads, hd2), lambda b, j, pt: (b, 0, 0))
    stat = pl.BlockSpec((1, n_heads, LANES), lambda b, j, pt: (b, 0, 0))

    def page_spec(p):
        return pl.BlockSpec((None, None, page, n_heads, hd2),
                            lambda b, j, pt: (layer, pt[b, first_page + j * pps + p], 0, 0, 0))

    grid_spec = pltpu.PrefetchScalarGridSpec(
        num_scalar_prefetch=1,
        grid=(db, n_pages // pps),
        in_specs=[pl.BlockSpec((None, hd2, LANES), lambda b, j, pt: (b, 0, 0)), tok, tok]
                 + [page_spec(p) for p in range(pps)] * 2,
        out_specs=(stat, stat, tok, tok),
    )
    return pl.pallas_call(
        functools.partial(_attn_sample_tc_kernel, pages_per_step=pps),
        out_shape=(jax.ShapeDtypeStruct((db, n_heads, LANES), F32), jax.ShapeDtypeStruct((db, n_heads, LANES), F32),
                   jax.ShapeDtypeStruct((db, n_heads, hd2), F32), jax.ShapeDtypeStruct((db, n_heads, hd2), F32)),
        grid_spec=grid_spec,
        compiler_params=_params("parallel", "arbitrary"),
        name="diff_attn_sample_tc",
    )(page_table, q2, k_new, v_new, *([cache_k] * pps), *([cache_v] * pps))


def _attn_sample_sc(q, cache_k, cache_v, page_table, layer, n_pages):
    db, n_heads, hd2 = q.shape
    hd = hd2 // 2
    depth, nphys, page = cache_k.shape[:3]
    cpp = page // SC_CHUNK
    n_chunks = n_pages * cpp
    assert db == SC_WORKERS and page % SC_CHUNK == 0 and n_chunks % 2 == 0 and page_table.shape[1] % SC_LANES == 0
    nv = hd // SC_LANES
    ck = cache_k.reshape(depth * nphys * page, n_heads, hd2)
    cv = cache_v.reshape(depth * nphys * page, n_heads, hd2)
    mesh = plsc.VectorSubcoreMesh(core_axis_name="c", subcore_axis_name="s",
                                  num_cores=SC_CORES, num_subcores=SC_SUBCORES)
    buf = pltpu.VMEM((SC_CHUNK, n_heads, hd2), F32)
    idx_t = pltpu.VMEM((SC_LANES,), jnp.int32)
    sem = pltpu.SemaphoreType.DMA

    @functools.partial(
        pl.kernel, mesh=mesh, name="diff_attn_sample_sc",
        compiler_params=pltpu.CompilerParams(use_tc_tiling_on_sc=True, needs_layout_passes=False),
        out_type=(jax.ShapeDtypeStruct((db, 2, n_heads, hd2), F32),
                  jax.ShapeDtypeStruct((db, 2 * n_heads, SC_LANES), F32),
                  jax.ShapeDtypeStruct((db, 2 * n_heads, SC_LANES), F32)),
        scratch_types=[buf, buf, buf, buf, pltpu.VMEM((n_heads, hd2), F32), pltpu.VMEM((2, n_heads, hd2), F32),
                       pltpu.VMEM((2 * n_heads, SC_LANES), F32), pltpu.VMEM((2 * n_heads, SC_LANES), F32),
                       pltpu.VMEM((page_table.shape[1],), jnp.int32), idx_t, idx_t, sem, sem, sem, sem],
    )
    def sc_kernel(q_hbm, ck_hbm, cv_hbm, pt_hbm, acc_hbm, m_hbm, l_hbm,
                  k0, k1, v0, v1, q_v, acc_v, m_v, l_v, pt_v, idx0, idx1, sk0, sk1, sv0, sv1):
        b = lax.axis_index("s") * SC_CORES + lax.axis_index("c")
        pltpu.sync_copy(pt_hbm.at[b], pt_v)
        pltpu.sync_copy(q_hbm.at[b], q_v)
        lanes = lax.iota(jnp.int32, SC_LANES)
        zero = jnp.zeros((SC_LANES,), F32)
        for r in range(2 * n_heads):
            m_v[r, :] = jnp.full((SC_LANES,), -jnp.inf, F32)
            l_v[r, :] = zero
        for mp in range(2):
            for h in range(n_heads):
                for i in range(hd2 // SC_LANES):
                    acc_v[mp, h, pl.ds(i * SC_LANES, SC_LANES)] = zero
        slots = ((k0, v0, idx0, sk0, sv0), (k1, v1, idx1, sk1, sv1))

        def copies(slot):
            kb, vb, idx, sk, sv = slots[slot]
            rows = idx.at[pl.ds(0, SC_CHUNK)]
            return (pltpu.make_async_copy(ck_hbm.at[rows], kb, sk), pltpu.make_async_copy(cv_hbm.at[rows], vb, sv))

        def issue(c, slot):
            j = c // cpp
            ptv = pt_v[pl.ds(pl.multiple_of((j // SC_LANES) * SC_LANES, SC_LANES), SC_LANES)]
            pg = ptv.at[jnp.full((SC_LANES,), j % SC_LANES, jnp.int32)].get(mode="promise_in_bounds")
            slots[slot][2][...] = (layer * nphys + pg) * page + (c % cpp) * SC_CHUNK + lanes
            for cpy in copies(slot):
                cpy.start()

        def wait(slot):
            for cpy in copies(slot):
                cpy.wait()

        def fold(slot):
            kb, vb = slots[slot][:2]

            @pl.loop(0, n_heads)
            def _(h):
                ps = []
                alphas = []
                for mp in range(2):
                    qs = [q_v[h, pl.ds(mp * hd + i * SC_LANES, SC_LANES)] for i in range(nv)]
                    svec = jnp.full((SC_LANES,), -jnp.inf, F32)
                    for t in range(SC_CHUNK):
                        a = qs[0] * kb[t, h, pl.ds(mp * hd, SC_LANES)]
                        for i in range(1, nv):
                            a = a + qs[i] * kb[t, h, pl.ds(mp * hd + i * SC_LANES, SC_LANES)]
                        svec = jnp.where(lanes == t, jnp.sum(a), svec)
                    row = mp * n_heads + h
                    m_old = m_v[row, :]
                    m_new = jnp.maximum(m_old, jnp.max(svec))
                    alpha = jnp.exp(m_old - m_new)
                    p = jnp.exp(svec - m_new)
                    l_v[row, :] = alpha * l_v[row, :] + jnp.sum(p)
                    m_v[row, :] = m_new
                    ps.append(p)
                    alphas.append(alpha)
                for i in range(hd2 // SC_LANES):
                    cols = pl.ds(i * SC_LANES, SC_LANES)
                    a1 = alphas[0] * acc_v[0, h, cols]
                    a2 = alphas[1] * acc_v[1, h, cols]
                    for t in range(SC_CHUNK):
                        v = vb[t, h, cols]
                        a1 = a1 + ps[0][t] * v
                        a2 = a2 + ps[1][t] * v
                    acc_v[0, h, cols] = a1
                    acc_v[1, h, cols] = a2

        issue(0, 0)

        @pl.loop(0, n_chunks, step=2)
        def _(c):
            issue(c + 1, 1)
            wait(0)
            fold(0)

            @pl.when(c + 2 < n_chunks)
            def _():
                issue(c + 2, 0)

            wait(1)
            fold(1)

        pltpu.sync_copy(acc_v, acc_hbm.at[b])
        pltpu.sync_copy(m_v, m_hbm.at[b])
        pltpu.sync_copy(l_v, l_hbm.at[b])

    return sc_kernel(q, ck, cv, page_table)


def _attn_finish_kernel(lq1_ref, lk1_ref, lq2_ref, lk2_ref, g_ref, *refs, lam_init, n_sc):
    o_ref = refs[-1]
    lam = _diff_lambda(lq1_ref, lk1_ref, lq2_ref, lk2_ref, lam_init)
    outs = []
    for mp in range(2):
        if n_sc:
            q_ref, kn_ref, vn_ref, as_ref, ms_ref, ls_ref = refs[:-1]
            n_heads, hd2 = q_ref.shape[1:]
            cols = slice(mp * hd2 // 2, (mp + 1) * hd2 // 2)
            rows = slice(mp * n_heads, (mp + 1) * n_heads)
            m_t = jnp.sum(q_ref[:, :, cols] * kn_ref[:, :, cols], axis=-1, keepdims=True) * LOG2_E
            m_s = ms_ref[:, rows, 0:1] * LOG2_E
            m = jnp.maximum(m_t, m_s)
            w_t = jnp.exp2(m_t - m)
            w_s = jnp.exp2(m_s - m)
            outs.append((w_t * vn_ref[...] + w_s * as_ref[:, mp]) / (w_t + w_s * ls_ref[:, rows, 0:1]))
        else:
            mt_ref, lt_ref, a1_ref, a2_ref = refs[:-1]
            n_heads = mt_ref.shape[1]
            lane = lax.broadcasted_iota(jnp.int32, mt_ref.shape, 2)
            head = lax.broadcasted_iota(jnp.int32, mt_ref.shape, 1)
            l_t = jnp.sum(jnp.where(lane == head + mp * n_heads, lt_ref[...], 0.0), axis=-1, keepdims=True)
            outs.append((a1_ref, a2_ref)[mp][...] / l_t)
    o = outs[0] - lam * outs[1]
    o_ref[...] = (_rms(o, g_ref[...]) * (1.0 - lam_init)).astype(o_ref.dtype)


def _attn_finish(state, lam_refs, subln_g, lam_init, n_sc):
    return pl.pallas_call(
        functools.partial(_attn_finish_kernel, lam_init=lam_init, n_sc=n_sc),
        out_shape=jax.ShapeDtypeStruct(state[2].shape, BF16),
        compiler_params=_params(),
        name="diff_attn_sample_finish",
    )(*lam_refs, subln_g, *state)


def _merge_oproj_kernel(ya_ref, ob_ref, ga_ref, gb_ref, x_ref, wc_ref, wa_ref, wo_ref, g_ref, h_ref, f_ref):
    y_a = _dot(ya_ref[...], wc_ref[...])
    y_b = _dot(ob_ref[...], wa_ref[...])
    mg = (ga_ref[...].astype(F32) * y_a + gb_ref[...].astype(F32) * y_b).astype(wo_ref.dtype)
    h = x_ref[...] + _dot(mg, wo_ref[...])
    h_ref[...] = h
    f_ref[...] = _rms(h, g_ref[...]).astype(f_ref.dtype)


def _merge_oproj(ya, ob, ga, gb, x, w_out_conv, w_out_attn, w_o, g):
    m, d = x.shape
    tm = _pick(m, 256)
    rows = lambda a: pl.BlockSpec((tm, a.shape[1]), lambda i: (i, 0))
    whole = lambda a: _resident(a.shape, lambda i: (0, 0))
    return pl.pallas_call(
        _merge_oproj_kernel,
        out_shape=(jax.ShapeDtypeStruct((m, d), F32), jax.ShapeDtypeStruct((m, d), BF16)),
        grid=(m // tm,),
        in_specs=[rows(ya), rows(ob), rows(ga), rows(gb), rows(x),
                  whole(w_out_conv), whole(w_out_attn), whole(w_o), pl.BlockSpec((1, d), lambda i: (0, 0))],
        out_specs=(pl.BlockSpec((tm, d), lambda i: (i, 0)), pl.BlockSpec((tm, d), lambda i: (i, 0))),
        compiler_params=_params("parallel"),
        name="merge_oproj",
    )(ya, ob, ga, gb, x, w_out_conv, w_out_attn, w_o, g.reshape(1, d))


def _ffn_kernel(f_ref, h_ref, wg_ref, wu_ref, wd_ref, g_ref, y_ref, *, final_norm):
    c = pl.program_id(1)

    @pl.when(c == 0)
    def _():
        y_ref[...] = h_ref[...]

    f = f_ref[...]
    a = jax.nn.silu(_dot(f, wg_ref[...])) * _dot(f, wu_ref[...])
    y_ref[...] += _dot(a.astype(wd_ref.dtype), wd_ref[...])

    if final_norm:
        @pl.when(c == pl.num_programs(1) - 1)
        def _():
            y_ref[...] = _rms(y_ref[...], g_ref[...])


def _ffn(f, h, w_gate, w_up, w_down, g_final, final_norm):
    m, d = h.shape
    dff = w_gate.shape[1]
    tm = _pick(m, 1024)
    tc = _pick(dff, 512)
    rows = pl.BlockSpec((tm, d), lambda i, c: (i, 0))
    return pl.pallas_call(
        functools.partial(_ffn_kernel, final_norm=final_norm),
        out_shape=jax.ShapeDtypeStruct((m, d), F32),
        grid=(m // tm, dff // tc),
        in_specs=[rows, rows,
                  pl.BlockSpec((d, tc), lambda i, c: (0, c)),
                  pl.BlockSpec((d, tc), lambda i, c: (0, c)),
                  pl.BlockSpec((tc, d), lambda i, c: (c, 0)),
                  pl.BlockSpec((1, d), lambda i, c: (0, 0))],
        out_specs=rows,
        compiler_params=_params("parallel", "arbitrary"),
        name="ffn",
    )(f, h, w_gate, w_up, w_down, g_final.reshape(1, d))


def kernel(x_prompt, x_sample, cache_k, cache_v, state_conv, page_table, w_in, conv_w, w_out_conv, lambda_q1, lambda_k1, lambda_q2, lambda_k2, subln_g, w_out_attn, w_o, norm_mix_g, norm_ffn_g, w_gate, w_up, w_down, norm_final_g):
    batch, seq, d = x_prompt.shape
    db, dseq, _ = x_sample.shape
    assert dseq == 1, "decode rows carry one new token per sequence"
    depth = w_in.shape[0]
    n_heads, hd2 = cache_k.shape[-2:]
    hd = hd2 // 2
    da = n_heads * hd2
    dc = conv_w.shape[1]
    assert state_conv.shape[2] == CONV_WIDTH - 1 and conv_w.shape[2] == CONV_WIDTH
    scale = hd ** -0.5
    qkv_col0 = 3 * dc
    gate_col0 = 3 * dc + 3 * da
    n_pages = page_table.shape[1]
    on_sc = db == SC_WORKERS and n_pages % SC_LANES == 0 and cache_k.shape[2] % SC_CHUNK == 0 and hd % SC_LANES == 0

    xp = x_prompt.reshape(batch * seq, d)
    xs = x_sample.reshape(db, d)
    outs = [[] for _ in range(6)]
    for l in range(depth):
        lam_init = 0.8 - 0.6 * math.exp(-0.3 * l)
        w_g_b = w_gate[l].astype(BF16)
        w_u_b = w_up[l].astype(BF16)
        w_d_b = w_down[l].astype(BF16)
        w_oc_b = w_out_conv[l].astype(BF16)
        w_oa_b = w_out_attn[l].astype(BF16)
        w_o_b = w_o[l].astype(BF16)
        cw_t = conv_w[l].T
        lam_refs = [a[l].reshape(1, hd) for a in (lambda_q1, lambda_k1, lambda_q2, lambda_k2)]
        sub_g = subln_g[l].reshape(1, hd2)
        st = state_conv[l]

        hn, hs = _norm_cast(xp, xs, norm_mix_g[l])
        q, k32, v32, kb, vb, q_s, k_s, v_s = _qkv_proj(hn, hs, w_in[l], qkv_col0, da, scale * LOG2_E, scale)
        to_heads = lambda a: a.reshape(db, n_heads, hd2)
        q_s3, k_s3, v_s3 = to_heads(q_s), to_heads(k_s), to_heads(v_s)
        if on_sc:
            state = (q_s3, k_s3, v_s3) + _attn_sample_sc(q_s3, cache_k, cache_v, page_table, l, n_pages)
        else:
            state = _attn_sample_tc(q_s3, k_s3, v_s3, cache_k, cache_v, page_table, l, 0)
        ya, ga, gb, conv_p, ya_s, ga_s, gb_s, u_s = _conv_inproj(hn, hs, st[:, 1, :], st[:, 0, :], w_in[l], cw_t,
                                                                 gate_col0, batch, seq)
        ob = _attn_prompt(q, kb, vb, lam_refs, sub_g, batch, seq, n_heads, lam_init)
        h, f = _merge_oproj(ya, ob, ga, gb, xp, w_oc_b, w_oa_b, w_o_b, norm_ffn_g[l])
        xp = _ffn(f, h, w_g_b, w_u_b, w_d_b, norm_final_g, l == depth - 1)
        ob_s = _attn_finish(state, lam_refs, sub_g, lam_init, n_pages if on_sc else 0)
        h_s, f_s = _merge_oproj(ya_s, ob_s.reshape(db, da), ga_s, gb_s, xs, w_oc_b, w_oa_b, w_o_b, norm_ffn_g[l])
        xs = _ffn(f_s, h_s, w_g_b, w_u_b, w_d_b, norm_final_g, l == depth - 1)

        outs[0].append(k32.reshape(batch, seq, n_heads, hd2))
        outs[1].append(v32.reshape(batch, seq, n_heads, hd2))
        outs[2].append(conv_p)
        outs[3].append(k_s.reshape(db, 1, n_heads, hd2))
        outs[4].append(v_s.reshape(db, 1, n_heads, hd2))
        outs[5].append(jnp.stack([st[:, 1, :], u_s], axis=1))

    y_prompt = xp.reshape(batch, seq, d)
    y_sample = xs.reshape(db, 1, d)
    return (y_prompt, y_sample) + tuple(jnp.stack(o) for o in outs)
```

```python
import functools
import math

import jax
import jax.numpy as jnp
from jax import lax
from jax.experimental import pallas as pl
from jax.experimental.pallas import tpu as pltpu
from jax.experimental.pallas import tpu_sc as plsc

F32 = jnp.float32
BF16 = jnp.bfloat16
EPS = 1e-6
MASK_VALUE = -1e30
CONV_WIDTH = 3
LOG2_E = math.log2(math.e)
V7X_VMEM_LIMIT_BYTES = 60 * 1024 * 1024
LANES = 128
SC_LANES = 16
SC_CORES = 2
SC_SUBCORES = 16
SC_WORKERS = SC_CORES * SC_SUBCORES
SC_CHUNK = 8


def _pick(n, pref):
    t = pref
    while t >= LANES:
        if n % t == 0:
            return t
        t //= 2
    return n


def _params(*sem):
    return pltpu.CompilerParams(dimension_semantics=sem, vmem_limit_bytes=V7X_VMEM_LIMIT_BYTES)


def _rms(x, g):
    return x * lax.rsqrt(jnp.mean(x * x, axis=-1, keepdims=True) + EPS) * g


def _dot(a, b):
    return jnp.dot(a, b, preferred_element_type=F32)


def _dot_nt(a, b):
    return lax.dot_general(a, b, (((1,), (1,)), ((), ())), preferred_element_type=F32)


def _resident(shape, index_map):
    return pl.BlockSpec(shape, index_map, pipeline_mode=pl.Buffered(1))


def _cast_weights(first_step, pairs):
    @pl.when(first_step)
    def _():
        for src, dst in pairs:
            dst[...] = src[...].astype(dst.dtype)


def _norm_kernel(x_ref, xs_ref, g_ref, *refs):
    n = (len(refs) - 2) // 2
    w_refs, (o_ref, os_ref), wb_refs = refs[:n], refs[n:n + 2], refs[n + 2:]
    o_ref[...] = _rms(x_ref[...], g_ref[...]).astype(o_ref.dtype)

    @pl.when(pl.program_id(0) == 0)
    def _():
        os_ref[...] = _rms(xs_ref[...], g_ref[...]).astype(os_ref.dtype)

    for w_ref, wb_ref in zip(w_refs, wb_refs):
        wb_ref[...] = w_ref[...].astype(wb_ref.dtype)


def _norm_cast(x, xs, g, weights):
    m, d = x.shape
    ms = xs.shape[0]
    tm = _pick(m, 512)
    steps = m // tm
    bf16_rows = 16
    riding = [w for w in weights if w.shape[0] % (steps * bf16_rows) == 0]
    slab = lambda w: pl.BlockSpec((w.shape[0] // steps, w.shape[1]), lambda i: (i, 0))
    outs = pl.pallas_call(
        _norm_kernel,
        out_shape=(jax.ShapeDtypeStruct((m, d), BF16), jax.ShapeDtypeStruct((ms, d), BF16),
                   *[jax.ShapeDtypeStruct(w.shape, BF16) for w in riding]),
        grid=(steps,),
        in_specs=[pl.BlockSpec((tm, d), lambda i: (i, 0)),
                  pl.BlockSpec((ms, d), lambda i: (0, 0)),
                  pl.BlockSpec((1, d), lambda i: (0, 0)),
                  *[slab(w) for w in riding]],
        out_specs=(pl.BlockSpec((tm, d), lambda i: (i, 0)),
                   pl.BlockSpec((ms, d), lambda i: (0, 0)),
                   *[slab(w) for w in riding]),
        compiler_params=_params("arbitrary"),
        name="norm_cast",
    )(x, xs, g.reshape(1, d), *riding)
    cast = iter(outs[2:])
    return outs[0], outs[1], [next(cast) if any(w is r for r in riding) else w.astype(BF16) for w in weights]


def _conv_taps(u, um1, um2, cw_ref):
    return um2 * cw_ref[0:1, :] + um1 * cw_ref[1:2, :] + u * cw_ref[2:3, :]


def _conv_kernel(hn_ref, hs_ref, um1_ref, um2_ref, wb_ref, wc_ref, wh_ref, wga_ref, wgb_ref, cw_ref,
                 ya_ref, ga_ref, gb_ref, tail_ref, yas_ref, gas_ref, gbs_ref, us_ref,
                 wbb, wcb, whb, wgab, wgbb, carry_ref, *, tiles_per_seq):
    i = pl.program_id(1)
    _cast_weights(i == 0, ((wb_ref, wbb), (wc_ref, wcb), (wh_ref, whb), (wga_ref, wgab), (wgb_ref, wgbb)))

    @pl.when(i == 0)
    def _():
        hs = hs_ref[...]
        us = _dot(hs, wcb[...]) * _dot(hs, whb[...])
        yas_ref[...] = (_dot(hs, wbb[...]) * _conv_taps(us, um1_ref[...], um2_ref[...], cw_ref)).astype(yas_ref.dtype)
        us_ref[...] = us
        gas_ref[...] = jax.nn.sigmoid(_dot(hs, wgab[...])).astype(gas_ref.dtype)
        gbs_ref[...] = jax.nn.sigmoid(_dot(hs, wgbb[...])).astype(gbs_ref.dtype)

    @pl.when(i % tiles_per_seq == 0)
    def _():
        carry_ref[...] = jnp.zeros_like(carry_ref)

    hn = hn_ref[...]
    u = _dot(hn, wcb[...]) * _dot(hn, whb[...])
    tm = u.shape[0]
    row = lax.broadcasted_iota(jnp.int32, u.shape, 0)
    c0 = carry_ref[0:1, :]
    c1 = carry_ref[1:2, :]
    um1 = jnp.where(row == 0, c1, pltpu.roll(u, 1, 0))
    um2 = jnp.where(row == 0, c0, jnp.where(row == 1, c1, pltpu.roll(u, 2, 0)))
    yc = _conv_taps(u, um1, um2, cw_ref)
    ya_ref[...] = (_dot(hn, wbb[...]) * yc).astype(ya_ref.dtype)
    last = u[tm - (CONV_WIDTH - 1):, :]
    carry_ref[...] = last
    tail_ref[0] = last
    ga_ref[...] = jax.nn.sigmoid(_dot(hn, wgab[...])).astype(ga_ref.dtype)
    gb_ref[...] = jax.nn.sigmoid(_dot(hn, wgbb[...])).astype(gb_ref.dtype)


def _conv_inproj(hn, hs, um1, um2, w_in, cw_t, gate_col0, batch, seq):
    m, d = hn.shape
    ms = hs.shape[0]
    dc = cw_t.shape[1]
    tm = _pick(seq, 1024)
    tn = _pick(dc, 512)
    nseg = dc // tn
    g0 = gate_col0 // tn
    wspec = lambda off: _resident((d, tn), lambda j, i: (0, off + j))
    blk = pl.BlockSpec((tm, tn), lambda j, i: (i, j))
    srow = pl.BlockSpec((ms, tn), lambda j, i: (0, j))
    act = lambda rows: jax.ShapeDtypeStruct((rows, dc), BF16)
    return pl.pallas_call(
        functools.partial(_conv_kernel, tiles_per_seq=seq // tm),
        out_shape=(act(m), act(m), act(m),
                   jax.ShapeDtypeStruct((batch, CONV_WIDTH - 1, dc), F32),
                   act(ms), act(ms), act(ms),
                   jax.ShapeDtypeStruct((ms, dc), F32)),
        grid=(nseg, m // tm),
        in_specs=[pl.BlockSpec((tm, d), lambda j, i: (i, 0)),
                  pl.BlockSpec((ms, d), lambda j, i: (0, 0)),
                  srow, srow,
                  wspec(0), wspec(nseg), wspec(2 * nseg), wspec(g0), wspec(g0 + nseg),
                  pl.BlockSpec((CONV_WIDTH, tn), lambda j, i: (0, j))],
        out_specs=(blk, blk, blk,
                   pl.BlockSpec((1, CONV_WIDTH - 1, tn), lambda j, i: (i // (seq // tm), 0, j)),
                   srow, srow, srow, srow),
        scratch_shapes=[pltpu.VMEM((d, tn), BF16)] * 5 + [pltpu.VMEM((CONV_WIDTH - 1, tn), F32)],
        compiler_params=_params("arbitrary", "arbitrary"),
        name="conv_inproj",
    )(hn, hs, um1, um2, w_in, w_in, w_in, w_in, w_in, cw_t)


def _qkv_kernel(hn_ref, hs_ref, wq_ref, wk_ref, wv_ref, q_ref, k_ref, v_ref, kb_ref, vb_ref,
                qs_ref, ks_ref, vs_ref, wqb, wkb, wvb, *, q_scale, qs_scale):
    i = pl.program_id(1)
    _cast_weights(i == 0, ((wq_ref, wqb), (wk_ref, wkb), (wv_ref, wvb)))

    @pl.when(i == 0)
    def _():
        hs = hs_ref[...]
        qs_ref[...] = _dot(hs, wqb[...]) * qs_scale
        ks_ref[...] = _dot(hs, wkb[...])
        vs_ref[...] = _dot(hs, wvb[...])

    hn = hn_ref[...]
    q_ref[...] = (_dot(hn, wqb[...]) * q_scale).astype(q_ref.dtype)
    k = _dot(hn, wkb[...])
    k_ref[...] = k
    kb_ref[...] = k.astype(kb_ref.dtype)
    v = _dot(hn, wvb[...])
    v_ref[...] = v
    vb_ref[...] = v.astype(vb_ref.dtype)


def _qkv_proj(hn, hs, w_in, col0, da, q_scale, qs_scale):
    m, d = hn.shape
    ms = hs.shape[0]
    tm = _pick(m, 1024)
    tn = _pick(da, 512)
    nseg = da // tn
    c0 = col0 // tn
    blk = pl.BlockSpec((tm, tn), lambda j, i: (i, j))
    srow = pl.BlockSpec((ms, tn), lambda j, i: (0, j))
    wspec = lambda off: _resident((d, tn), lambda j, i: (0, c0 + off + j))
    return pl.pallas_call(
        functools.partial(_qkv_kernel, q_scale=q_scale, qs_scale=qs_scale),
        out_shape=(jax.ShapeDtypeStruct((m, da), BF16),
                   jax.ShapeDtypeStruct((m, da), F32),
                   jax.ShapeDtypeStruct((m, da), F32),
                   jax.ShapeDtypeStruct((m, da), BF16),
                   jax.ShapeDtypeStruct((m, da), BF16),
                   jax.ShapeDtypeStruct((ms, da), F32),
                   jax.ShapeDtypeStruct((ms, da), F32),
                   jax.ShapeDtypeStruct((ms, da), F32)),
        grid=(nseg, m // tm),
        in_specs=[pl.BlockSpec((tm, d), lambda j, i: (i, 0)),
                  pl.BlockSpec((ms, d), lambda j, i: (0, 0)),
                  wspec(0), wspec(nseg), wspec(2 * nseg)],
        out_specs=(blk, blk, blk, blk, blk, srow, srow, srow),
        scratch_shapes=[pltpu.VMEM((d, tn), BF16)] * 3,
        compiler_params=_params("arbitrary", "arbitrary"),
        name="qkv_proj",
    )(hn, hs, w_in, w_in, w_in)


def _diff_lambda(lq1_ref, lk1_ref, lq2_ref, lk2_ref, lam_init):
    a = jnp.sum(lq1_ref[...] * lk1_ref[...], axis=-1, keepdims=True)
    b = jnp.sum(lq2_ref[...] * lk2_ref[...], axis=-1, keepdims=True)
    return jnp.exp(a) - jnp.exp(b) + lam_init


def _attn_prompt_kernel(lq1_ref, lk1_ref, lq2_ref, lk2_ref, g_ref, q_ref, k_ref, v_ref, o_ref,
                        s_ref, p_ref, c_ref, *, tq, rc, lam_init):
    seq = q_ref.shape[0]
    hd = q_ref.shape[1] // 2
    nq = seq // tq
    lam = _diff_lambda(lq1_ref, lk1_ref, lq2_ref, lk2_ref, lam_init)

    def scores(qi):
        slot = qi % 2
        lo, hi = qi * tq, (qi + 1) * tq
        q = q_ref[lo:hi, :]
        for mp in range(2):
            cols = slice(mp * hd, (mp + 1) * hd)
            if qi > 0:
                s_ref[slot, mp, :, :lo] = _dot_nt(q[:, cols], k_ref[:lo, cols])
            sd = _dot_nt(q[:, cols], k_ref[lo:hi, cols])
            keep = (lax.broadcasted_iota(jnp.int32, sd.shape, 1) <= lax.broadcasted_iota(jnp.int32, sd.shape, 0))
            s_ref[slot, mp, :, lo:hi] = jnp.where(keep, sd, MASK_VALUE)

    def softmax_diff(qi):
        slot = qi % 2
        hi = (qi + 1) * tq
        for r in range(tq // rc):
            rows = slice(r * rc, (r + 1) * rc)
            x1 = s_ref[slot, 0, rows, :hi]
            x2 = s_ref[slot, 1, rows, :hi]
            e1 = jnp.exp2(x1 - jnp.max(x1, axis=-1, keepdims=True))
            e2 = jnp.exp2(x2 - jnp.max(x2, axis=-1, keepdims=True))
            l1 = jnp.sum(e1, axis=-1, keepdims=True)
            l2 = jnp.sum(e2, axis=-1, keepdims=True)
            p_ref[slot, rows, :hi] = (e1 - e2 * (lam * l1 / l2)).astype(p_ref.dtype)
            c_ref[slot, rows, :] = 1.0 / l1

    def values(qi):
        slot = qi % 2
        lo, hi = qi * tq, (qi + 1) * tq
        o = _dot(p_ref[slot, :, :hi], v_ref[:hi, :]) * c_ref[slot]
        o_ref[lo:hi, :] = (_rms(o, g_ref[...]) * (1.0 - lam_init)).astype(o_ref.dtype)

    scores(0)
    for qi in range(nq):
        if qi + 1 < nq:
            scores(qi + 1)
        softmax_diff(qi)
        values(qi)


def _attn_prompt(q, kb, vb, lam_refs, subln_g, batch, seq, n_heads, lam_init):
    m, da = q.shape
    hd2 = da // n_heads
    tq = _pick(seq, 256)
    rc = 32
    vec = lambda n: pl.BlockSpec((1, n), lambda b, h: (0, 0))
    head = pl.BlockSpec((seq, hd2), lambda b, h: (b, h))
    return pl.pallas_call(
        functools.partial(_attn_prompt_kernel, tq=tq, rc=rc, lam_init=lam_init),
        out_shape=jax.ShapeDtypeStruct((m, da), BF16),
        grid=(batch, n_heads),
        in_specs=[vec(hd2 // 2)] * 4 + [vec(hd2), head, head, head],
        out_specs=head,
        scratch_shapes=[pltpu.VMEM((2, 2, tq, seq), F32), pltpu.VMEM((2, tq, seq), BF16),
                        pltpu.VMEM((2, tq, 1), F32)],
        compiler_params=_params("parallel", "parallel"),
        name="diff_attn_prompt",
    )(*lam_refs, subln_g, q, kb, vb)


def _attn_sample_tc_kernel(pt_ref, q2_ref, kn_ref, vn_ref, *rest, pages_per_step):
    del pt_ref
    k_refs = rest[:pages_per_step]
    v_refs = rest[pages_per_step:2 * pages_per_step]
    m_ref, l_ref, acc1_ref, acc2_ref = (r.at[0] for r in rest[2 * pages_per_step:])
    j = pl.program_id(1)
    n_heads, hd2 = kn_ref.shape[-2:]
    q2 = q2_ref[...]
    lane = lax.broadcasted_iota(jnp.int32, (n_heads, LANES), 1)
    head = lax.broadcasted_iota(jnp.int32, (n_heads, LANES), 0)
    own = (lane == head, lane == head + n_heads)
    ones = jnp.ones((LANES, LANES), BF16)

    def spread(x, mp):
        lead = x.shape[:-2]
        sel = jnp.where(own[mp], x, 0.0).astype(BF16).reshape((-1, LANES))
        return _dot(sel, ones).reshape(lead + (n_heads, LANES))

    def spread_f32(x, mp):
        hi = x.astype(BF16).astype(F32)
        return spread(hi, mp) + spread(x - hi, mp)

    def tile2(x):
        return jnp.concatenate([x, x], axis=-1)

    @pl.when(j == 0)
    def _():
        m_ref[...] = _dot(kn_ref[0].astype(BF16), q2)
        l_ref[...] = jnp.ones_like(l_ref)
        acc1_ref[...] = vn_ref[0]
        acc2_ref[...] = vn_ref[0]

    page = k_refs[0].shape[0]
    ss = [_dot(k_ref[...].reshape(page * n_heads, hd2).astype(BF16), q2).reshape(page, n_heads, LANES)
          for k_ref in k_refs]
    m_old = m_ref[...]
    m_new = m_old
    for s in ss:
        m_new = jnp.maximum(m_new, jnp.max(s, axis=0))
    alpha = jnp.exp2(m_old - m_new)
    ps = [jnp.exp2(s - m_new[None]) for s in ss]
    l_new = alpha * l_ref[...]
    for p in ps:
        l_new = l_new + jnp.sum(p, axis=0)
    l_ref[...] = l_new
    m_ref[...] = m_new
    for mp, acc_ref in enumerate((acc1_ref, acc2_ref)):
        acc = tile2(spread_f32(alpha, mp)) * acc_ref[...]
        for p, v_ref in zip(ps, v_refs):
            acc = acc + jnp.sum(tile2(spread(p, mp)) * v_ref[...], axis=0)
        acc_ref[...] = acc


def _attn_sample_tc(q, k_new, v_new, cache_k, cache_v, page_table, layer, first_page):
    db, n_heads, hd2 = q.shape
    hd = hd2 // 2
    page = cache_k.shape[2]
    n_pages = page_table.shape[1] - first_page
    pps = 8 if n_pages % 8 == 0 else 1
    assert 2 * n_heads <= LANES and n_pages > 0
    qt = (q * LOG2_E).reshape(db, n_heads, 2, hd).transpose(0, 2, 3, 1)
    q2 = jnp.zeros((db, 2, hd, 2, n_heads), q.dtype)
    q2 = q2.at[:, 0, :, 0, :].set(qt[:, 0]).at[:, 1, :, 1, :].set(qt[:, 1])
    q2 = jnp.pad(q2.reshape(db, hd2, 2 * n_heads), ((0, 0), (0, 0), (0, LANES - 2 * n_heads))).astype(BF16)
    tok = pl.BlockSpec((1, n_heads, hd2), lambda b, j, pt: (b, 0, 0))
    stat = pl.BlockSpec((1, n_heads, LANES), lambda b, j, pt: (b, 0, 0))

    def page_spec(p):
        return pl.BlockSpec((None, None, page, n_heads, hd2),
                            lambda b, j, pt: (layer, pt[b, first_page + j * pps + p], 0, 0, 0))

    grid_spec = pltpu.PrefetchScalarGridSpec(
        num_scalar_prefetch=1,
        grid=(db, n_pages // pps),
        in_specs=[pl.BlockSpec((None, hd2, LANES), lambda b, j, pt: (b, 0, 0)), tok, tok]
                 + [page_spec(p) for p in range(pps)] * 2,
        out_specs=(stat, stat, tok, tok),
    )
    return pl.pallas_call(
        functools.partial(_attn_sample_tc_kernel, pages_per_step=pps),
        out_shape=(jax.ShapeDtypeStruct((db, n_heads, LANES), F32), jax.ShapeDtypeStruct((db, n_heads, LANES), F32),
                   jax.ShapeDtypeStruct((db, n_heads, hd2), F32), jax.ShapeDtypeStruct((db, n_heads, hd2), F32)),
        grid_spec=grid_spec,
        compiler_params=_params("parallel", "arbitrary"),
        name="diff_attn_sample_tc",
    )(page_table, q2, k_new, v_new, *([cache_k] * pps), *([cache_v] * pps))


def _attn_sample_sc(q, cache_k, cache_v, page_table, layer, n_pages):
    db, n_heads, hd2 = q.shape
    hd = hd2 // 2
    depth, nphys, page = cache_k.shape[:3]
    cpp = page // SC_CHUNK
    n_chunks = n_pages * cpp
    assert db == SC_WORKERS and page % SC_CHUNK == 0 and n_chunks % 2 == 0 and page_table.shape[1] % SC_LANES == 0
    nv = hd // SC_LANES
    ck = cache_k.reshape(depth * nphys * page, n_heads, hd2)
    cv = cache_v.reshape(depth * nphys * page, n_heads, hd2)
    mesh = plsc.VectorSubcoreMesh(core_axis_name="c", subcore_axis_name="s",
                                  num_cores=SC_CORES, num_subcores=SC_SUBCORES)
    buf = pltpu.VMEM((SC_CHUNK, n_heads, hd2), F32)
    idx_t = pltpu.VMEM((SC_LANES,), jnp.int32)
    sem = pltpu.SemaphoreType.DMA

    @functools.partial(
        pl.kernel, mesh=mesh, name="diff_attn_sample_sc",
        compiler_params=pltpu.CompilerParams(use_tc_tiling_on_sc=True, needs_layout_passes=False),
        out_type=(jax.ShapeDtypeStruct((db, 2, n_heads, hd2), F32),
                  jax.ShapeDtypeStruct((db, 2 * n_heads, SC_LANES), F32),
                  jax.ShapeDtypeStruct((db, 2 * n_heads, SC_LANES), F32)),
        scratch_types=[buf, buf, buf, buf, pltpu.VMEM((n_heads, hd2), F32), pltpu.VMEM((2, n_heads, hd2), F32),
                       pltpu.VMEM((2 * n_heads, SC_LANES), F32), pltpu.VMEM((2 * n_heads, SC_LANES), F32),
                       pltpu.VMEM((page_table.shape[1],), jnp.int32), idx_t, idx_t, sem, sem, sem, sem],
    )
    def sc_kernel(q_hbm, ck_hbm, cv_hbm, pt_hbm, acc_hbm, m_hbm, l_hbm,
                  k0, k1, v0, v1, q_v, acc_v, m_v, l_v, pt_v, idx0, idx1, sk0, sk1, sv0, sv1):
        b = lax.axis_index("s") * SC_CORES + lax.axis_index("c")
        pltpu.sync_copy(pt_hbm.at[b], pt_v)
        pltpu.sync_copy(q_hbm.at[b], q_v)
        lanes = lax.iota(jnp.int32, SC_LANES)
        zero = jnp.zeros((SC_LANES,), F32)
        for r in range(2 * n_heads):
            m_v[r, :] = jnp.full((SC_LANES,), -jnp.inf, F32)
            l_v[r, :] = zero
        for mp in range(2):
            for h in range(n_heads):
                for i in range(hd2 // SC_LANES):
                    acc_v[mp, h, pl.ds(i * SC_LANES, SC_LANES)] = zero
        slots = ((k0, v0, idx0, sk0, sv0), (k1, v1, idx1, sk1, sv1))

        def copies(slot):
            kb, vb, idx, sk, sv = slots[slot]
            rows = idx.at[pl.ds(0, SC_CHUNK)]
            return (pltpu.make_async_copy(ck_hbm.at[rows], kb, sk), pltpu.make_async_copy(cv_hbm.at[rows], vb, sv))

        def issue(c, slot):
            j = c // cpp
            ptv = pt_v[pl.ds(pl.multiple_of((j // SC_LANES) * SC_LANES, SC_LANES), SC_LANES)]
            pg = ptv.at[jnp.full((SC_LANES,), j % SC_LANES, jnp.int32)].get(mode="promise_in_bounds")
            slots[slot][2][...] = (layer * nphys + pg) * page + (c % cpp) * SC_CHUNK + lanes
            for cpy in copies(slot):
                cpy.start()

        def wait(slot):
            for cpy in copies(slot):
                cpy.wait()

        def fold(slot):
            kb, vb = slots[slot][:2]

            @pl.loop(0, n_heads)
            def _(h):
                ps = []
                alphas = []
                for mp in range(2):
                    qs = [q_v[h, pl.ds(mp * hd + i * SC_LANES, SC_LANES)] for i in range(nv)]
                    svec = jnp.full((SC_LANES,), -jnp.inf, F32)
                    for t in range(SC_CHUNK):
                        a = qs[0] * kb[t, h, pl.ds(mp * hd, SC_LANES)]
                        for i in range(1, nv):
                            a = a + qs[i] * kb[t, h, pl.ds(mp * hd + i * SC_LANES, SC_LANES)]
                        svec = jnp.where(lanes == t, jnp.sum(a), svec)
                    row = mp * n_heads + h
                    m_old = m_v[row, :]
                    m_new = jnp.maximum(m_old, jnp.max(svec))
                    alpha = jnp.exp(m_old - m_new)
                    p = jnp.exp(svec - m_new)
                    l_v[row, :] = alpha * l_v[row, :] + jnp.sum(p)
                    m_v[row, :] = m_new
                    ps.append(p)
                    alphas.append(alpha)
                for i in range(hd2 // SC_LANES):
                    cols = pl.ds(i * SC_LANES, SC_LANES)
                    a1 = alphas[0] * acc_v[0, h, cols]
                    a2 = alphas[1] * acc_v[1, h, cols]
                    for t in range(SC_CHUNK):
                        v = vb[t, h, cols]
                        a1 = a1 + ps[0][t] * v
                        a2 = a2 + ps[1][t] * v
                    acc_v[0, h, cols] = a1
                    acc_v[1, h, cols] = a2

        issue(0, 0)

        @pl.loop(0, n_chunks, step=2)
        def _(c):
            issue(c + 1, 1)
            wait(0)
            fold(0)

            @pl.when(c + 2 < n_chunks)
            def _():
                issue(c + 2, 0)

            wait(1)
            fold(1)

        pltpu.sync_copy(acc_v, acc_hbm.at[b])
        pltpu.sync_copy(m_v, m_hbm.at[b])
        pltpu.sync_copy(l_v, l_hbm.at[b])

    return sc_kernel(q, ck, cv, page_table)


def _attn_finish_kernel(lq1_ref, lk1_ref, lq2_ref, lk2_ref, g_ref, *refs, lam_init, n_sc):
    o_ref = refs[-1]
    lam = _diff_lambda(lq1_ref, lk1_ref, lq2_ref, lk2_ref, lam_init)
    outs = []
    for mp in range(2):
        if n_sc:
            q_ref, kn_ref, vn_ref, as_ref, ms_ref, ls_ref = refs[:-1]
            n_heads, hd2 = q_ref.shape[1:]
            cols = slice(mp * hd2 // 2, (mp + 1) * hd2 // 2)
            rows = slice(mp * n_heads, (mp + 1) * n_heads)
            m_t = jnp.sum(q_ref[:, :, cols] * kn_ref[:, :, cols], axis=-1, keepdims=True) * LOG2_E
            m_s = ms_ref[:, rows, 0:1] * LOG2_E
            m = jnp.maximum(m_t, m_s)
            w_t = jnp.exp2(m_t - m)
            w_s = jnp.exp2(m_s - m)
            outs.append((w_t * vn_ref[...] + w_s * as_ref[:, mp]) / (w_t + w_s * ls_ref[:, rows, 0:1]))
        else:
            mt_ref, lt_ref, a1_ref, a2_ref = refs[:-1]
            n_heads = mt_ref.shape[1]
            lane = lax.broadcasted_iota(jnp.int32, mt_ref.shape, 2)
            head = lax.broadcasted_iota(jnp.int32, mt_ref.shape, 1)
            l_t = jnp.sum(jnp.where(lane == head + mp * n_heads, lt_ref[...], 0.0), axis=-1, keepdims=True)
            outs.append((a1_ref, a2_ref)[mp][...] / l_t)
    o = outs[0] - lam * outs[1]
    o_ref[...] = (_rms(o, g_ref[...]) * (1.0 - lam_init)).astype(o_ref.dtype)


def _attn_finish(state, lam_refs, subln_g, lam_init, n_sc):
    return pl.pallas_call(
        functools.partial(_attn_finish_kernel, lam_init=lam_init, n_sc=n_sc),
        out_shape=jax.ShapeDtypeStruct(state[2].shape, BF16),
        compiler_params=_params(),
        name="diff_attn_sample_finish",
    )(*lam_refs, subln_g, *state)


def _merge_oproj_kernel(ya_ref, ob_ref, ga_ref, gb_ref, x_ref, wc_ref, wa_ref, wo_ref, g_ref, h_ref, f_ref):
    y_a = _dot(ya_ref[...], wc_ref[...])
    y_b = _dot(ob_ref[...], wa_ref[...])
    mg = (ga_ref[...].astype(F32) * y_a + gb_ref[...].astype(F32) * y_b).astype(wo_ref.dtype)
    h = x_ref[...] + _dot(mg, wo_ref[...])
    h_ref[...] = h
    f_ref[...] = _rms(h, g_ref[...]).astype(f_ref.dtype)


def _merge_oproj(ya, ob, ga, gb, x, w_out_conv, w_out_attn, w_o, g):
    m, d = x.shape
    tm = _pick(m, 256)
    rows = lambda a: pl.BlockSpec((tm, a.shape[1]), lambda i: (i, 0))
    whole = lambda a: _resident(a.shape, lambda i: (0, 0))
    return pl.pallas_call(
        _merge_oproj_kernel,
        out_shape=(jax.ShapeDtypeStruct((m, d), F32), jax.ShapeDtypeStruct((m, d), BF16)),
        grid=(m // tm,),
        in_specs=[rows(ya), rows(ob), rows(ga), rows(gb), rows(x),
                  whole(w_out_conv), whole(w_out_attn), whole(w_o), pl.BlockSpec((1, d), lambda i: (0, 0))],
        out_specs=(pl.BlockSpec((tm, d), lambda i: (i, 0)), pl.BlockSpec((tm, d), lambda i: (i, 0))),
        compiler_params=_params("parallel"),
        name="merge_oproj",
    )(ya, ob, ga, gb, x, w_out_conv, w_out_attn, w_o, g.reshape(1, d))


def _ffn_kernel(f_ref, h_ref, wg_ref, wu_ref, wd_ref, g_ref, y_ref, *, final_norm):
    c = pl.program_id(1)

    @pl.when(c == 0)
    def _():
        y_ref[...] = h_ref[...]

    f = f_ref[...]
    a = jax.nn.silu(_dot(f, wg_ref[...])) * _dot(f, wu_ref[...])
    y_ref[...] += _dot(a.astype(wd_ref.dtype), wd_ref[...])

    if final_norm:
        @pl.when(c == pl.num_programs(1) - 1)
        def _():
            y_ref[...] = _rms(y_ref[...], g_ref[...])


def _ffn(f, h, w_gate, w_up, w_down, g_final, final_norm):
    m, d = h.shape
    dff = w_gate.shape[1]
    tm = _pick(m, 1024)
    tc = _pick(dff, 512)
    rows = pl.BlockSpec((tm, d), lambda i, c: (i, 0))
    return pl.pallas_call(
        functools.partial(_ffn_kernel, final_norm=final_norm),
        out_shape=jax.ShapeDtypeStruct((m, d), F32),
        grid=(m // tm, dff // tc),
        in_specs=[rows, rows,
                  pl.BlockSpec((d, tc), lambda i, c: (0, c)),
                  pl.BlockSpec((d, tc), lambda i, c: (0, c)),
                  pl.BlockSpec((tc, d), lambda i, c: (c, 0)),
                  pl.BlockSpec((1, d), lambda i, c: (0, 0))],
        out_specs=rows,
        compiler_params=_params("parallel", "arbitrary"),
        name="ffn",
    )(f, h, w_gate, w_up, w_down, g_final.reshape(1, d))


def kernel(x_prompt, x_sample, cache_k, cache_v, state_conv, page_table, w_in, conv_w, w_out_conv, lambda_q1, lambda_k1, lambda_q2, lambda_k2, subln_g, w_out_attn, w_o, norm_mix_g, norm_ffn_g, w_gate, w_up, w_down, norm_final_g):
    batch, seq, d = x_prompt.shape
    db, dseq, _ = x_sample.shape
    assert dseq == 1, "decode rows carry one new token per sequence"
    depth = w_in.shape[0]
    n_heads, hd2 = cache_k.shape[-2:]
    hd = hd2 // 2
    da = n_heads * hd2
    dc = conv_w.shape[1]
    assert state_conv.shape[2] == CONV_WIDTH - 1 and conv_w.shape[2] == CONV_WIDTH
    scale = hd ** -0.5
    qkv_col0 = 3 * dc
    gate_col0 = 3 * dc + 3 * da
    n_pages = page_table.shape[1]
    on_sc = db == SC_WORKERS and n_pages % SC_LANES == 0 and cache_k.shape[2] % SC_CHUNK == 0 and hd % SC_LANES == 0

    xp = x_prompt.reshape(batch * seq, d)
    xs = x_sample.reshape(db, d)
    outs = [[] for _ in range(6)]
    for l in range(depth):
        lam_init = 0.8 - 0.6 * math.exp(-0.3 * l)
        cw_t = conv_w[l].T
        lam_refs = [a[l].reshape(1, hd) for a in (lambda_q1, lambda_k1, lambda_q2, lambda_k2)]
        sub_g = subln_g[l].reshape(1, hd2)
        st = state_conv[l]

        hn, hs, (w_g_b, w_u_b, w_d_b, w_oc_b, w_oa_b, w_o_b) = _norm_cast(
            xp, xs, norm_mix_g[l], [w_gate[l], w_up[l], w_down[l], w_out_conv[l], w_out_attn[l], w_o[l]])
        q, k32, v32, kb, vb, q_s, k_s, v_s = _qkv_proj(hn, hs, w_in[l], qkv_col0, da, scale * LOG2_E, scale)
        to_heads = lambda a: a.reshape(db, n_heads, hd2)
        q_s3, k_s3, v_s3 = to_heads(q_s), to_heads(k_s), to_heads(v_s)
        if on_sc:
            state = (q_s3, k_s3, v_s3) + _attn_sample_sc(q_s3, cache_k, cache_v, page_table, l, n_pages)
        else:
            state = _attn_sample_tc(q_s3, k_s3, v_s3, cache_k, cache_v, page_table, l, 0)
        ya, ga, gb, conv_p, ya_s, ga_s, gb_s, u_s = _conv_inproj(hn, hs, st[:, 1, :], st[:, 0, :], w_in[l], cw_t,
                                                                 gate_col0, batch, seq)
        ob = _attn_prompt(q, kb, vb, lam_refs, sub_g, batch, seq, n_heads, lam_init)
        h, f = _merge_oproj(ya, ob, ga, gb, xp, w_oc_b, w_oa_b, w_o_b, norm_ffn_g[l])
        xp = _ffn(f, h, w_g_b, w_u_b, w_d_b, norm_final_g, l == depth - 1)
        ob_s = _attn_finish(state, lam_refs, sub_g, lam_init, n_pages if on_sc else 0)
        h_s, f_s = _merge_oproj(ya_s, ob_s.reshape(db, da), ga_s, gb_s, xs, w_oc_b, w_oa_b, w_o_b, norm_ffn_g[l])
        xs = _ffn(f_s, h_s, w_g_b, w_u_b, w_d_b, norm_final_g, l == depth - 1)

        outs[0].append(k32.reshape(batch, seq, n_heads, hd2))
        outs[1].append(v32.reshape(batch, seq, n_heads, hd2))
        outs[2].append(conv_p)
        outs[3].append(k_s.reshape(db, 1, n_heads, hd2))
        outs[4].append(v_s.reshape(db, 1, n_heads, hd2))
        outs[5].append(jnp.stack([st[:, 1, :], u_s], axis=1))

    y_prompt = xp.reshape(batch, seq, d)
    y_sample = xs.reshape(db, 1, d)
    return (y_prompt, y_sample) + tuple(jnp.stack(o) for o in outs)
```

```python
import functools
import math

import jax
import jax.numpy as jnp
from jax import lax
from jax.experimental import pallas as pl
from jax.experimental.pallas import tpu as pltpu
from jax.experimental.pallas import tpu_sc as plsc

F32 = jnp.float32
BF16 = jnp.bfloat16
EPS = 1e-6
MASK_VALUE = -1e30
CONV_WIDTH = 3
LOG2_E = math.log2(math.e)
V7X_VMEM_LIMIT_BYTES = 60 * 1024 * 1024
LANES = 128
SC_LANES = 16
SC_CORES = 2
SC_SUBCORES = 16
SC_WORKERS = SC_CORES * SC_SUBCORES
SC_CHUNK = 8


def _pick(n, pref):
    t = pref
    while t >= LANES:
        if n % t == 0:
            return t
        t //= 2
    return n


def _params(*sem):
    return pltpu.CompilerParams(dimension_semantics=sem, vmem_limit_bytes=V7X_VMEM_LIMIT_BYTES)


def _rms(x, g):
    return x * lax.rsqrt(jnp.mean(x * x, axis=-1, keepdims=True) + EPS) * g


def _dot(a, b):
    return jnp.dot(a, b, preferred_element_type=F32)


def _dot_nt(a, b):
    return lax.dot_general(a, b, (((1,), (1,)), ((), ())), preferred_element_type=F32)


def _resident(shape, index_map):
    return pl.BlockSpec(shape, index_map, pipeline_mode=pl.Buffered(1))


def _cast_weights(first_step, pairs):
    @pl.when(first_step)
    def _():
        for src, dst in pairs:
            dst[...] = src[...].astype(dst.dtype)


def _norm_kernel(x_ref, xs_ref, g_ref, *refs):
    n = (len(refs) - 2) // 2
    w_refs, (o_ref, os_ref), wb_refs = refs[:n], refs[n:n + 2], refs[n + 2:]
    o_ref[...] = _rms(x_ref[...], g_ref[...]).astype(o_ref.dtype)

    @pl.when(pl.program_id(0) == 0)
    def _():
        os_ref[...] = _rms(xs_ref[...], g_ref[...]).astype(os_ref.dtype)

    for w_ref, wb_ref in zip(w_refs, wb_refs):
        wb_ref[...] = w_ref[...].astype(wb_ref.dtype)


def _norm_cast(x, xs, g, weights):
    m, d = x.shape
    ms = xs.shape[0]
    tm = _pick(m, 512)
    steps = m // tm
    bf16_rows = 16
    riding = [w for w in weights if w.shape[0] % (steps * bf16_rows) == 0]
    slab = lambda w: pl.BlockSpec((w.shape[0] // steps, w.shape[1]), lambda i: (i, 0))
    outs = pl.pallas_call(
        _norm_kernel,
        out_shape=(jax.ShapeDtypeStruct((m, d), BF16), jax.ShapeDtypeStruct((ms, d), BF16),
                   *[jax.ShapeDtypeStruct(w.shape, BF16) for w in riding]),
        grid=(steps,),
        in_specs=[pl.BlockSpec((tm, d), lambda i: (i, 0)),
                  pl.BlockSpec((ms, d), lambda i: (0, 0)),
                  pl.BlockSpec((1, d), lambda i: (0, 0)),
                  *[slab(w) for w in riding]],
        out_specs=(pl.BlockSpec((tm, d), lambda i: (i, 0)),
                   pl.BlockSpec((ms, d), lambda i: (0, 0)),
                   *[slab(w) for w in riding]),
        compiler_params=_params("arbitrary"),
        name="norm_cast",
    )(x, xs, g.reshape(1, d), *riding)
    cast = iter(outs[2:])
    return outs[0], outs[1], [next(cast) if any(w is r for r in riding) else w.astype(BF16) for w in weights]


def _conv_taps(u, um1, um2, cw_ref):
    return um2 * cw_ref[0:1, :] + um1 * cw_ref[1:2, :] + u * cw_ref[2:3, :]


def _conv_kernel(hn_ref, hs_ref, um1_ref, um2_ref, wb_ref, wc_ref, wh_ref, wga_ref, wgb_ref, cw_ref,
                 ya_ref, ga_ref, gb_ref, tail_ref, yas_ref, gas_ref, gbs_ref, us_ref,
                 wbb, wcb, whb, wgab, wgbb, carry_ref, *, tiles_per_seq):
    i = pl.program_id(1)
    _cast_weights(i == 0, ((wb_ref, wbb), (wc_ref, wcb), (wh_ref, whb), (wga_ref, wgab), (wgb_ref, wgbb)))

    @pl.when(i == 0)
    def _():
        hs = hs_ref[...]
        us = _dot(hs, wcb[...]) * _dot(hs, whb[...])
        yas_ref[...] = (_dot(hs, wbb[...]) * _conv_taps(us, um1_ref[...], um2_ref[...], cw_ref)).astype(yas_ref.dtype)
        us_ref[...] = us
        gas_ref[...] = jax.nn.sigmoid(_dot(hs, wgab[...])).astype(gas_ref.dtype)
        gbs_ref[...] = jax.nn.sigmoid(_dot(hs, wgbb[...])).astype(gbs_ref.dtype)

    @pl.when(i % tiles_per_seq == 0)
    def _():
        carry_ref[...] = jnp.zeros_like(carry_ref)

    hn = hn_ref[...]
    u = _dot(hn, wcb[...]) * _dot(hn, whb[...])
    tm = u.shape[0]
    row = lax.broadcasted_iota(jnp.int32, u.shape, 0)
    c0 = carry_ref[0:1, :]
    c1 = carry_ref[1:2, :]
    um1 = jnp.where(row == 0, c1, pltpu.roll(u, 1, 0))
    um2 = jnp.where(row == 0, c0, jnp.where(row == 1, c1, pltpu.roll(u, 2, 0)))
    yc = _conv_taps(u, um1, um2, cw_ref)
    ya_ref[...] = (_dot(hn, wbb[...]) * yc).astype(ya_ref.dtype)
    last = u[tm - (CONV_WIDTH - 1):, :]
    carry_ref[...] = last
    tail_ref[0] = last
    ga_ref[...] = jax.nn.sigmoid(_dot(hn, wgab[...])).astype(ga_ref.dtype)
    gb_ref[...] = jax.nn.sigmoid(_dot(hn, wgbb[...])).astype(gb_ref.dtype)


def _conv_inproj(hn, hs, um1, um2, w_in, cw_t, gate_col0, batch, seq):
    m, d = hn.shape
    ms = hs.shape[0]
    dc = cw_t.shape[1]
    tm = _pick(seq, 1024)
    tn = _pick(dc, 512)
    nseg = dc // tn
    g0 = gate_col0 // tn
    wspec = lambda off: _resident((d, tn), lambda j, i: (0, off + j))
    blk = pl.BlockSpec((tm, tn), lambda j, i: (i, j))
    srow = pl.BlockSpec((ms, tn), lambda j, i: (0, j))
    act = lambda rows: jax.ShapeDtypeStruct((rows, dc), BF16)
    return pl.pallas_call(
        functools.partial(_conv_kernel, tiles_per_seq=seq // tm),
        out_shape=(act(m), act(m), act(m),
                   jax.ShapeDtypeStruct((batch, CONV_WIDTH - 1, dc), F32),
                   act(ms), act(ms), act(ms),
                   jax.ShapeDtypeStruct((ms, dc), F32)),
        grid=(nseg, m // tm),
        in_specs=[pl.BlockSpec((tm, d), lambda j, i: (i, 0)),
                  pl.BlockSpec((ms, d), lambda j, i: (0, 0)),
                  srow, srow,
                  wspec(0), wspec(nseg), wspec(2 * nseg), wspec(g0), wspec(g0 + nseg),
                  pl.BlockSpec((CONV_WIDTH, tn), lambda j, i: (0, j))],
        out_specs=(blk, blk, blk,
                   pl.BlockSpec((1, CONV_WIDTH - 1, tn), lambda j, i: (i // (seq // tm), 0, j)),
                   srow, srow, srow, srow),
        scratch_shapes=[pltpu.VMEM((d, tn), BF16)] * 5 + [pltpu.VMEM((CONV_WIDTH - 1, tn), F32)],
        compiler_params=_params("arbitrary", "arbitrary"),
        name="conv_inproj",
    )(hn, hs, um1, um2, w_in, w_in, w_in, w_in, w_in, cw_t)


def _qkv_kernel(hn_ref, hs_ref, wq_ref, wk_ref, wv_ref, q_ref, k_ref, v_ref, kb_ref, vb_ref,
                qs_ref, ks_ref, vs_ref, wqb, wkb, wvb, *, q_scale, qs_scale):
    i = pl.program_id(1)
    _cast_weights(i == 0, ((wq_ref, wqb), (wk_ref, wkb), (wv_ref, wvb)))

    @pl.when(i == 0)
    def _():
        hs = hs_ref[...]
        qs_ref[...] = _dot(hs, wqb[...]) * qs_scale
        ks_ref[...] = _dot(hs, wkb[...])
        vs_ref[...] = _dot(hs, wvb[...])

    hn = hn_ref[...]
    q_ref[...] = (_dot(hn, wqb[...]) * q_scale).astype(q_ref.dtype)
    k = _dot(hn, wkb[...])
    k_ref[...] = k
    kb_ref[...] = k.astype(kb_ref.dtype)
    v = _dot(hn, wvb[...])
    v_ref[...] = v
    vb_ref[...] = v.astype(vb_ref.dtype)


def _qkv_proj(hn, hs, w_in, col0, da, q_scale, qs_scale):
    m, d = hn.shape
    ms = hs.shape[0]
    tm = _pick(m, 1024)
    tn = _pick(da, 512)
    nseg = da // tn
    c0 = col0 // tn
    blk = pl.BlockSpec((tm, tn), lambda j, i: (i, j))
    srow = pl.BlockSpec((ms, tn), lambda j, i: (0, j))
    wspec = lambda off: _resident((d, tn), lambda j, i: (0, c0 + off + j))
    return pl.pallas_call(
        functools.partial(_qkv_kernel, q_scale=q_scale, qs_scale=qs_scale),
        out_shape=(jax.ShapeDtypeStruct((m, da), BF16),
                   jax.ShapeDtypeStruct((m, da), F32),
                   jax.ShapeDtypeStruct((m, da), F32),
                   jax.ShapeDtypeStruct((m, da), BF16),
                   jax.ShapeDtypeStruct((m, da), BF16),
                   jax.ShapeDtypeStruct((ms, da), F32),
                   jax.ShapeDtypeStruct((ms, da), F32),
                   jax.ShapeDtypeStruct((ms, da), F32)),
        grid=(nseg, m // tm),
        in_specs=[pl.BlockSpec((tm, d), lambda j, i: (i, 0)),
                  pl.BlockSpec((ms, d), lambda j, i: (0, 0)),
                  wspec(0), wspec(nseg), wspec(2 * nseg)],
        out_specs=(blk, blk, blk, blk, blk, srow, srow, srow),
        scratch_shapes=[pltpu.VMEM((d, tn), BF16)] * 3,
        compiler_params=_params("arbitrary", "arbitrary"),
        name="qkv_proj",
    )(hn, hs, w_in, w_in, w_in)


def _diff_lambda(lq1_ref, lk1_ref, lq2_ref, lk2_ref, lam_init):
    a = jnp.sum(lq1_ref[...] * lk1_ref[...], axis=-1, keepdims=True)
    b = jnp.sum(lq2_ref[...] * lk2_ref[...], axis=-1, keepdims=True)
    return jnp.exp(a) - jnp.exp(b) + lam_init


def _attn_prompt_kernel(lq1_ref, lk1_ref, lq2_ref, lk2_ref, g_ref, q_ref, k_ref, v_ref, o_ref,
                        s_ref, p_ref, c_ref, *, tq, rc, lam_init):
    seq = q_ref.shape[0]
    hd = q_ref.shape[1] // 2
    nq = seq // tq
    lam = _diff_lambda(lq1_ref, lk1_ref, lq2_ref, lk2_ref, lam_init)

    def scores(qi):
        slot = qi % 2
        lo, hi = qi * tq, (qi + 1) * tq
        q = q_ref[lo:hi, :]
        for mp in range(2):
            cols = slice(mp * hd, (mp + 1) * hd)
            if qi > 0:
                s_ref[slot, mp, :, :lo] = _dot_nt(q[:, cols], k_ref[:lo, cols])
            sd = _dot_nt(q[:, cols], k_ref[lo:hi, cols])
            keep = (lax.broadcasted_iota(jnp.int32, sd.shape, 1) <= lax.broadcasted_iota(jnp.int32, sd.shape, 0))
            s_ref[slot, mp, :, lo:hi] = jnp.where(keep, sd, MASK_VALUE)

    def softmax_diff(qi):
        slot = qi % 2
        hi = (qi + 1) * tq
        for r in range(tq // rc):
            rows = slice(r * rc, (r + 1) * rc)
            x1 = s_ref[slot, 0, rows, :hi]
            x2 = s_ref[slot, 1, rows, :hi]
            e1 = jnp.exp2(x1 - jnp.max(x1, axis=-1, keepdims=True))
            e2 = jnp.exp2(x2 - jnp.max(x2, axis=-1, keepdims=True))
            l1 = jnp.sum(e1, axis=-1, keepdims=True)
            l2 = jnp.sum(e2, axis=-1, keepdims=True)
            p_ref[slot, rows, :hi] = (e1 - e2 * (lam * l1 / l2)).astype(p_ref.dtype)
            c_ref[slot, rows, :] = 1.0 / l1

    def values(qi):
        slot = qi % 2
        lo, hi = qi * tq, (qi + 1) * tq
        o = _dot(p_ref[slot, :, :hi], v_ref[:hi, :]) * c_ref[slot]
        o_ref[lo:hi, :] = (_rms(o, g_ref[...]) * (1.0 - lam_init)).astype(o_ref.dtype)

    scores(0)
    for qi in range(nq):
        if qi + 1 < nq:
            scores(qi + 1)
        softmax_diff(qi)
        values(qi)


def _attn_prompt(q, kb, vb, lam_refs, subln_g, batch, seq, n_heads, lam_init):
    m, da = q.shape
    hd2 = da // n_heads
    tq = _pick(seq, 256)
    rc = 32
    vec = lambda n: pl.BlockSpec((1, n), lambda b, h: (0, 0))
    head = pl.BlockSpec((seq, hd2), lambda b, h: (b, h))
    return pl.pallas_call(
        functools.partial(_attn_prompt_kernel, tq=tq, rc=rc, lam_init=lam_init),
        out_shape=jax.ShapeDtypeStruct((m, da), BF16),
        grid=(batch, n_heads),
        in_specs=[vec(hd2 // 2)] * 4 + [vec(hd2), head, head, head],
        out_specs=head,
        scratch_shapes=[pltpu.VMEM((2, 2, tq, seq), F32), pltpu.VMEM((2, tq, seq), BF16),
                        pltpu.VMEM((2, tq, 1), F32)],
        compiler_params=_params("parallel", "parallel"),
        name="diff_attn_prompt",
    )(*lam_refs, subln_g, q, kb, vb)


def _attn_sample_tc_kernel(pt_ref, q2_ref, kn_ref, vn_ref, *rest, pages_per_step):
    del pt_ref
    k_refs = rest[:pages_per_step]
    v_refs = rest[pages_per_step:2 * pages_per_step]
    m_ref, l_ref, acc1_ref, acc2_ref = (r.at[0] for r in rest[2 * pages_per_step:])
    j = pl.program_id(1)
    n_heads, hd2 = kn_ref.shape[-2:]
    q2 = q2_ref[...]
    lane = lax.broadcasted_iota(jnp.int32, (n_heads, LANES), 1)
    head = lax.broadcasted_iota(jnp.int32, (n_heads, LANES), 0)
    own = (lane == head, lane == head + n_heads)
    ones = jnp.ones((LANES, LANES), BF16)

    def spread(x, mp):
        lead = x.shape[:-2]
        sel = jnp.where(own[mp], x, 0.0).astype(BF16).reshape((-1, LANES))
        return _dot(sel, ones).reshape(lead + (n_heads, LANES))

    def spread_f32(x, mp):
        hi = x.astype(BF16).astype(F32)
        return spread(hi, mp) + spread(x - hi, mp)

    def tile2(x):
        return jnp.concatenate([x, x], axis=-1)

    @pl.when(j == 0)
    def _():
        m_ref[...] = _dot(kn_ref[0].astype(BF16), q2)
        l_ref[...] = jnp.ones_like(l_ref)
        acc1_ref[...] = vn_ref[0]
        acc2_ref[...] = vn_ref[0]

    page = k_refs[0].shape[0]
    ss = [_dot(k_ref[...].reshape(page * n_heads, hd2).astype(BF16), q2).reshape(page, n_heads, LANES)
          for k_ref in k_refs]
    m_old = m_ref[...]
    m_new = m_old
    for s in ss:
        m_new = jnp.maximum(m_new, jnp.max(s, axis=0))
    alpha = jnp.exp2(m_old - m_new)
    ps = [jnp.exp2(s - m_new[None]) for s in ss]
    l_new = alpha * l_ref[...]
    for p in ps:
        l_new = l_new + jnp.sum(p, axis=0)
    l_ref[...] = l_new
    m_ref[...] = m_new
    for mp, acc_ref in enumerate((acc1_ref, acc2_ref)):
        acc = tile2(spread_f32(alpha, mp)) * acc_ref[...]
        for p, v_ref in zip(ps, v_refs):
            acc = acc + jnp.sum(tile2(spread(p, mp)) * v_ref[...], axis=0)
        acc_ref[...] = acc


def _attn_sample_tc(q, k_new, v_new, cache_k, cache_v, page_table, layer, first_page):
    db, n_heads, hd2 = q.shape
    hd = hd2 // 2
    page = cache_k.shape[2]
    n_pages = page_table.shape[1] - first_page
    pps = 8 if n_pages % 8 == 0 else 1
    assert 2 * n_heads <= LANES and n_pages > 0
    qt = (q * LOG2_E).reshape(db, n_heads, 2, hd).transpose(0, 2, 3, 1)
    q2 = jnp.zeros((db, 2, hd, 2, n_heads), q.dtype)
    q2 = q2.at[:, 0, :, 0, :].set(qt[:, 0]).at[:, 1, :, 1, :].set(qt[:, 1])
    q2 = jnp.pad(q2.reshape(db, hd2, 2 * n_heads), ((0, 0), (0, 0), (0, LANES - 2 * n_heads))).astype(BF16)
    tok = pl.BlockSpec((1, n_heads, hd2), lambda b, j, pt: (b, 0, 0))
    stat = pl.BlockSpec((1, n_heads, LANES), lambda b, j, pt: (b, 0, 0))

    def page_spec(p):
        return pl.BlockSpec((None, None, page, n_heads, hd2),
                            lambda b, j, pt: (layer, pt[b, first_page + j * pps + p], 0, 0, 0))

    grid_spec = pltpu.PrefetchScalarGridSpec(
        num_scalar_prefetch=1,
        grid=(db, n_pages // pps),
        in_specs=[pl.BlockSpec((None, hd2, LANES), lambda b, j, pt: (b, 0, 0)), tok, tok]
                 + [page_spec(p) for p in range(pps)] * 2,
        out_specs=(stat, stat, tok, tok),
    )
    return pl.pallas_call(
        functools.partial(_attn_sample_tc_kernel, pages_per_step=pps),
        out_shape=(jax.ShapeDtypeStruct((db, n_heads, LANES), F32), jax.ShapeDtypeStruct((db, n_heads, LANES), F32),
                   jax.ShapeDtypeStruct((db, n_heads, hd2), F32), jax.ShapeDtypeStruct((db, n_heads, hd2), F32)),
        grid_spec=grid_spec,
        compiler_params=_params("parallel", "arbitrary"),
        name="diff_attn_sample_tc",
    )(page_table, q2, k_new, v_new, *([cache_k] * pps), *([cache_v] * pps))


def _attn_sample_sc(q, cache_k, cache_v, page_table, layer, n_pages):
    db, n_heads, hd2 = q.shape
    hd = hd2 // 2
    depth, nphys, page = cache_k.shape[:3]
    cpp = page // SC_CHUNK
    n_chunks = n_pages * cpp
    assert db == SC_WORKERS and page % SC_CHUNK == 0 and n_chunks % 2 == 0 and page_table.shape[1] % SC_LANES == 0
    nv = hd // SC_LANES
    ck = cache_k.reshape(depth * nphys * page, n_heads, hd2)
    cv = cache_v.reshape(depth * nphys * page, n_heads, hd2)
    mesh = plsc.VectorSubcoreMesh(core_axis_name="c", subcore_axis_name="s",
                                  num_cores=SC_CORES, num_subcores=SC_SUBCORES)
    buf = pltpu.VMEM((SC_CHUNK, n_heads, hd2), F32)
    idx_t = pltpu.VMEM((SC_LANES,), jnp.int32)
    sem = pltpu.SemaphoreType.DMA

    @functools.partial(
        pl.kernel, mesh=mesh, name="diff_attn_sample_sc",
        compiler_params=pltpu.CompilerParams(use_tc_tiling_on_sc=True, needs_layout_passes=False),
        out_type=(jax.ShapeDtypeStruct((db, 2, n_heads, hd2), F32),
                  jax.ShapeDtypeStruct((db, 2 * n_heads, SC_LANES), F32),
                  jax.ShapeDtypeStruct((db, 2 * n_heads, SC_LANES), F32)),
        scratch_types=[buf, buf, buf, buf, pltpu.VMEM((n_heads, hd2), F32), pltpu.VMEM((2, n_heads, hd2), F32),
                       pltpu.VMEM((2 * n_heads, SC_LANES), F32), pltpu.VMEM((2 * n_heads, SC_LANES), F32),
                       pltpu.VMEM((page_table.shape[1],), jnp.int32), idx_t, idx_t, sem, sem, sem, sem],
    )
    def sc_kernel(q_hbm, ck_hbm, cv_hbm, pt_hbm, acc_hbm, m_hbm, l_hbm,
                  k0, k1, v0, v1, q_v, acc_v, m_v, l_v, pt_v, idx0, idx1, sk0, sk1, sv0, sv1):
        b = lax.axis_index("s") * SC_CORES + lax.axis_index("c")
        pltpu.sync_copy(pt_hbm.at[b], pt_v)
        pltpu.sync_copy(q_hbm.at[b], q_v)
        lanes = lax.iota(jnp.int32, SC_LANES)
        zero = jnp.zeros((SC_LANES,), F32)
        for r in range(2 * n_heads):
            m_v[r, :] = jnp.full((SC_LANES,), -jnp.inf, F32)
            l_v[r, :] = zero
        for mp in range(2):
            for h in range(n_heads):
                for i in range(hd2 // SC_LANES):
                    acc_v[mp, h, pl.ds(i * SC_LANES, SC_LANES)] = zero
        slots = ((k0, v0, idx0, sk0, sv0), (k1, v1, idx1, sk1, sv1))

        def copies(slot):
            kb, vb, idx, sk, sv = slots[slot]
            rows = idx.at[pl.ds(0, SC_CHUNK)]
            return (pltpu.make_async_copy(ck_hbm.at[rows], kb, sk), pltpu.make_async_copy(cv_hbm.at[rows], vb, sv))

        def issue(c, slot):
            j = c // cpp
            ptv = pt_v[pl.ds(pl.multiple_of((j // SC_LANES) * SC_LANES, SC_LANES), SC_LANES)]
            pg = ptv.at[jnp.full((SC_LANES,), j % SC_LANES, jnp.int32)].get(mode="promise_in_bounds")
            slots[slot][2][...] = (layer * nphys + pg) * page + (c % cpp) * SC_CHUNK + lanes
            for cpy in copies(slot):
                cpy.start()

        def wait(slot):
            for cpy in copies(slot):
                cpy.wait()

        def fold(slot):
            kb, vb = slots[slot][:2]

            @pl.loop(0, n_heads)
            def _(h):
                ps = []
                alphas = []
                for mp in range(2):
                    qs = [q_v[h, pl.ds(mp * hd + i * SC_LANES, SC_LANES)] for i in range(nv)]
                    svec = jnp.full((SC_LANES,), -jnp.inf, F32)
                    for t in range(SC_CHUNK):
                        a = qs[0] * kb[t, h, pl.ds(mp * hd, SC_LANES)]
                        for i in range(1, nv):
                            a = a + qs[i] * kb[t, h, pl.ds(mp * hd + i * SC_LANES, SC_LANES)]
                        svec = jnp.where(lanes == t, jnp.sum(a), svec)
                    row = mp * n_heads + h
                    m_old = m_v[row, :]
                    m_new = jnp.maximum(m_old, jnp.max(svec))
                    alpha = jnp.exp(m_old - m_new)
                    p = jnp.exp(svec - m_new)
                    l_v[row, :] = alpha * l_v[row, :] + jnp.sum(p)
                    m_v[row, :] = m_new
                    ps.append(p)
                    alphas.append(alpha)
                for i in range(hd2 // SC_LANES):
                    cols = pl.ds(i * SC_LANES, SC_LANES)
                    a1 = alphas[0] * acc_v[0, h, cols]
                    a2 = alphas[1] * acc_v[1, h, cols]
                    for t in range(SC_CHUNK):
                        v = vb[t, h, cols]
                        a1 = a1 + ps[0][t] * v
                        a2 = a2 + ps[1][t] * v
                    acc_v[0, h, cols] = a1
                    acc_v[1, h, cols] = a2

        issue(0, 0)

        @pl.loop(0, n_chunks, step=2)
        def _(c):
            issue(c + 1, 1)
            wait(0)
            fold(0)

            @pl.when(c + 2 < n_chunks)
            def _():
                issue(c + 2, 0)

            wait(1)
            fold(1)

        pltpu.sync_copy(acc_v, acc_hbm.at[b])
        pltpu.sync_copy(m_v, m_hbm.at[b])
        pltpu.sync_copy(l_v, l_hbm.at[b])

    return sc_kernel(q, ck, cv, page_table)


def _attn_finish_kernel(lq1_ref, lk1_ref, lq2_ref, lk2_ref, g_ref, *refs, lam_init, on_sc):
    o_ref = refs[-1]
    lam = _diff_lambda(lq1_ref, lk1_ref, lq2_ref, lk2_ref, lam_init)
    outs = []
    for mp in range(2):
        if on_sc:
            q_ref, kn_ref, vn_ref, as_ref, ms_ref, ls_ref = refs[:-1]
            n_heads, hd2 = q_ref.shape[1:]
            cols = slice(mp * hd2 // 2, (mp + 1) * hd2 // 2)
            rows = slice(mp * n_heads, (mp + 1) * n_heads)
            m_t = jnp.sum(q_ref[:, :, cols] * kn_ref[:, :, cols], axis=-1, keepdims=True) * LOG2_E
            m_s = ms_ref[:, rows, 0:1] * LOG2_E
            m = jnp.maximum(m_t, m_s)
            w_t = jnp.exp2(m_t - m)
            w_s = jnp.exp2(m_s - m)
            outs.append((w_t * vn_ref[...] + w_s * as_ref[:, mp]) / (w_t + w_s * ls_ref[:, rows, 0:1]))
        else:
            mt_ref, lt_ref, a1_ref, a2_ref = refs[:-1]
            n_heads = mt_ref.shape[1]
            lane = lax.broadcasted_iota(jnp.int32, mt_ref.shape, 2)
            head = lax.broadcasted_iota(jnp.int32, mt_ref.shape, 1)
            l_t = jnp.sum(jnp.where(lane == head + mp * n_heads, lt_ref[...], 0.0), axis=-1, keepdims=True)
            outs.append((a1_ref, a2_ref)[mp][...] / l_t)
    o = outs[0] - lam * outs[1]
    o_ref[...] = (_rms(o, g_ref[...]) * (1.0 - lam_init)).astype(o_ref.dtype)


def _attn_finish(state, lam_refs, subln_g, lam_init, on_sc):
    return pl.pallas_call(
        functools.partial(_attn_finish_kernel, lam_init=lam_init, on_sc=on_sc),
        out_shape=jax.ShapeDtypeStruct(state[2].shape, BF16),
        compiler_params=_params(),
        name="diff_attn_sample_finish",
    )(*lam_refs, subln_g, *state)


def _merge_oproj_kernel(ya_ref, ob_ref, ga_ref, gb_ref, x_ref, wc_ref, wa_ref, wo_ref, g_ref, h_ref, f_ref):
    y_a = _dot(ya_ref[...], wc_ref[...])
    y_b = _dot(ob_ref[...], wa_ref[...])
    mg = (ga_ref[...].astype(F32) * y_a + gb_ref[...].astype(F32) * y_b).astype(wo_ref.dtype)
    h = x_ref[...] + _dot(mg, wo_ref[...])
    h_ref[...] = h
    f_ref[...] = _rms(h, g_ref[...]).astype(f_ref.dtype)


def _merge_oproj(ya, ob, ga, gb, x, w_out_conv, w_out_attn, w_o, g):
    m, d = x.shape
    tm = _pick(m, 256)
    rows = lambda a: pl.BlockSpec((tm, a.shape[1]), lambda i: (i, 0))
    whole = lambda a: _resident(a.shape, lambda i: (0, 0))
    return pl.pallas_call(
        _merge_oproj_kernel,
        out_shape=(jax.ShapeDtypeStruct((m, d), F32), jax.ShapeDtypeStruct((m, d), BF16)),
        grid=(m // tm,),
        in_specs=[rows(ya), rows(ob), rows(ga), rows(gb), rows(x),
                  whole(w_out_conv), whole(w_out_attn), whole(w_o), pl.BlockSpec((1, d), lambda i: (0, 0))],
        out_specs=(pl.BlockSpec((tm, d), lambda i: (i, 0)), pl.BlockSpec((tm, d), lambda i: (i, 0))),
        compiler_params=_params("parallel"),
        name="merge_oproj",
    )(ya, ob, ga, gb, x, w_out_conv, w_out_attn, w_o, g.reshape(1, d))


def _ffn_kernel(f_ref, h_ref, wg_ref, wu_ref, wd_ref, g_ref, y_ref, *, final_norm):
    c = pl.program_id(1)

    @pl.when(c == 0)
    def _():
        y_ref[...] = h_ref[...]

    f = f_ref[...]
    a = jax.nn.silu(_dot(f, wg_ref[...])) * _dot(f, wu_ref[...])
    y_ref[...] += _dot(a.astype(wd_ref.dtype), wd_ref[...])

    if final_norm:
        @pl.when(c == pl.num_programs(1) - 1)
        def _():
            y_ref[...] = _rms(y_ref[...], g_ref[...])


def _ffn(f, h, w_gate, w_up, w_down, g_final, final_norm):
    m, d = h.shape
    dff = w_gate.shape[1]
    tm = _pick(m, 1024)
    tc = _pick(dff, 512)
    rows = pl.BlockSpec((tm, d), lambda i, c: (i, 0))
    return pl.pallas_call(
        functools.partial(_ffn_kernel, final_norm=final_norm),
        out_shape=jax.ShapeDtypeStruct((m, d), F32),
        grid=(m // tm, dff // tc),
        in_specs=[rows, rows,
                  pl.BlockSpec((d, tc), lambda i, c: (0, c)),
                  pl.BlockSpec((d, tc), lambda i, c: (0, c)),
                  pl.BlockSpec((tc, d), lambda i, c: (c, 0)),
                  pl.BlockSpec((1, d), lambda i, c: (0, 0))],
        out_specs=rows,
        compiler_params=_params("parallel", "arbitrary"),
        name="ffn",
    )(f, h, w_gate, w_up, w_down, g_final.reshape(1, d))


def kernel(x_prompt, x_sample, cache_k, cache_v, state_conv, page_table, w_in, conv_w, w_out_conv, lambda_q1, lambda_k1, lambda_q2, lambda_k2, subln_g, w_out_attn, w_o, norm_mix_g, norm_ffn_g, w_gate, w_up, w_down, norm_final_g):
    batch, seq, d = x_prompt.shape
    db, dseq, _ = x_sample.shape
    assert dseq == 1, "decode rows carry one new token per sequence"
    depth = w_in.shape[0]
    n_heads, hd2 = cache_k.shape[-2:]
    hd = hd2 // 2
    da = n_heads * hd2
    dc = conv_w.shape[1]
    assert state_conv.shape[2] == CONV_WIDTH - 1 and conv_w.shape[2] == CONV_WIDTH
    scale = hd ** -0.5
    qkv_col0 = 3 * dc
    gate_col0 = 3 * dc + 3 * da
    n_pages = page_table.shape[1]
    on_sc = db == SC_WORKERS and n_pages % SC_LANES == 0 and cache_k.shape[2] % SC_CHUNK == 0 and hd % SC_LANES == 0

    xp = x_prompt.reshape(batch * seq, d)
    xs = x_sample.reshape(db, d)
    outs = [[] for _ in range(6)]
    for l in range(depth):
        lam_init = 0.8 - 0.6 * math.exp(-0.3 * l)
        cw_t = conv_w[l].T
        lam_refs = [a[l].reshape(1, hd) for a in (lambda_q1, lambda_k1, lambda_q2, lambda_k2)]
        sub_g = subln_g[l].reshape(1, hd2)
        st = state_conv[l]

        hn, hs, (w_g_b, w_u_b, w_d_b, w_oc_b, w_oa_b, w_o_b) = _norm_cast(
            xp, xs, norm_mix_g[l], [w_gate[l], w_up[l], w_down[l], w_out_conv[l], w_out_attn[l], w_o[l]])
        q, k32, v32, kb, vb, q_s, k_s, v_s = _qkv_proj(hn, hs, w_in[l], qkv_col0, da, scale * LOG2_E, scale)
        to_heads = lambda a: a.reshape(db, n_heads, hd2)
        q_s3, k_s3, v_s3 = to_heads(q_s), to_heads(k_s), to_heads(v_s)
        if on_sc:
            state = (q_s3, k_s3, v_s3) + _attn_sample_sc(q_s3, cache_k, cache_v, page_table, l, n_pages)
        else:
            state = _attn_sample_tc(q_s3, k_s3, v_s3, cache_k, cache_v, page_table, l, 0)
        ya, ga, gb, conv_p, ya_s, ga_s, gb_s, u_s = _conv_inproj(hn, hs, st[:, 1, :], st[:, 0, :], w_in[l], cw_t,
                                                                 gate_col0, batch, seq)
        ob = _attn_prompt(q, kb, vb, lam_refs, sub_g, batch, seq, n_heads, lam_init)
        h, f = _merge_oproj(ya, ob, ga, gb, xp, w_oc_b, w_oa_b, w_o_b, norm_ffn_g[l])
        xp = _ffn(f, h, w_g_b, w_u_b, w_d_b, norm_final_g, l == depth - 1)
        ob_s = _attn_finish(state, lam_refs, sub_g, lam_init, on_sc)
        h_s, f_s = _merge_oproj(ya_s, ob_s.reshape(db, da), ga_s, gb_s, xs, w_oc_b, w_oa_b, w_o_b, norm_ffn_g[l])
        xs = _ffn(f_s, h_s, w_g_b, w_u_b, w_d_b, norm_final_g, l == depth - 1)

        outs[0].append(k32.reshape(batch, seq, n_heads, hd2))
        outs[1].append(v32.reshape(batch, seq, n_heads, hd2))
        outs[2].append(conv_p)
        outs[3].append(k_s.reshape(db, 1, n_heads, hd2))
        outs[4].append(v_s.reshape(db, 1, n_heads, hd2))
        outs[5].append(jnp.stack([st[:, 1, :], u_s], axis=1))

    y_prompt = xp.reshape(batch, seq, d)
    y_sample = xs.reshape(db, 1, d)
    return (y_prompt, y_sample) + tuple(jnp.stack(o) for o in outs)
```
